```python
import math
import jax, jax.numpy as jnp
from jax import lax
import numpy as np

D_MODEL = 1024
BATCH = 8
SEQ = 2048
DEPTH = 1
DEC_BATCH = 128
DEC_SEQ = 4
PAST_LEN = 16384
PAGE_SIZE = 128

D_MIX = 2 * D_MODEL
D_SSM = D_MIX // 2
D_S5 = D_MIX - D_SSM
SSM_HEAD_DIM = 64
SSM_HEADS = D_SSM // SSM_HEAD_DIM
SSM_GROUPS = 2
SSM_STATE = 128
CONV_W = 4
SSD_CHUNK = 64
D_CONV = D_SSM + 2 * SSM_GROUPS * SSM_STATE
S5_GROUP_CH = 16
S5_GROUPS = D_S5 // S5_GROUP_CH
S5_STATE = 64
D_IN_PROJ = D_SSM + D_CONV + SSM_HEADS + D_S5
MEM_TOKENS = 256
MEM_HEADS = 4
MEM_HEAD_DIM = D_MODEL // MEM_HEADS
N_EXPERTS = 64
N_EXPERT_GROUPS = 8
TOPK_GROUPS = 4
TOP_K = 6
D_EXPERT = 256
D_SHARED = 256
ROUTE_SCALE = 2.5
MOE_BLOCK = 512
ALPHA = (2.0 * DEPTH) ** 0.25
BETA = (8.0 * DEPTH) ** -0.25
LN_EPS = 1e-5
RMS_EPS = 1e-5

kernel_name = 'hymba_ssd_s5_memxattn_moe_step'


def layer_norm(x, g, b):
    xf = x.astype(jnp.float32)
    mu = jnp.mean(xf, -1, keepdims=True)
    xc = xf - mu
    var = jnp.mean(xc * xc, -1, keepdims=True)
    return (xc * lax.rsqrt(var + LN_EPS) * g + b).astype(x.dtype)


def causal_conv(x, prev, w, bias):
    l = x.shape[1]
    xp = jnp.concatenate([prev, x], axis=1)
    out = bias + sum(w[k] * xp[:, k:k + l] for k in range(CONV_W))
    return out, xp[:, -(CONV_W - 1):]


def ssd(x, dt, a, bm, cm, h0):
    bsz, l, h, p = x.shape
    g, n = bm.shape[2], bm.shape[3]
    r = h // g
    q = math.gcd(l, SSD_CHUNK)
    c = l // q
    x = x.reshape(bsz, c, q, g, r, p)
    dt = dt.reshape(bsz, c, q, g, r)
    bm = bm.reshape(bsz, c, q, g, n)
    cm = cm.reshape(bsz, c, q, g, n)
    acum = jnp.cumsum(dt * a.reshape(g, r), axis=2)
    seg = acum[:, :, :, None] - acum[:, :, None, :]
    causal = jnp.tril(jnp.ones((q, q), bool))[:, :, None, None]
    lmat = jnp.exp(jnp.where(causal, seg, -jnp.inf))
    cb = jnp.einsum('bctgn,bcsgn->bctsg', cm, bm)
    y_diag = jnp.einsum('bctsg,bctsgr,bcsgr,bcsgrp->bctgrp', cb, lmat, dt, x)
    decay_end = jnp.exp(acum[:, :, -1:] - acum)
    chunk_states = jnp.einsum('bcsgn,bcsgr,bcsgrp->bcgrpn', bm, decay_end * dt, x)
    chunk_decay = jnp.exp(acum[:, :, -1])

    def step(hc, inp):
        st, dec = inp
        return dec[..., None, None] * hc + st, hc

    h_final, h_prev = lax.scan(step, h0.reshape(bsz, g, r, p, n),
                               (jnp.moveaxis(chunk_states, 1, 0), jnp.moveaxis(chunk_decay, 1, 0)))
    h_prev = jnp.moveaxis(h_prev, 0, 1)
    y_off = jnp.einsum('bctgn,bcgrpn,bctgr->bctgrp', cm, h_prev, jnp.exp(acum))
    y = (y_diag + y_off).reshape(bsz, l, h, p)
    return y, h_final.reshape(bsz, h, p, n)


def mamba_mixer(z, xbc, dt_raw, conv_st, ssm_st, conv_w, conv_b, dt_bias, a_log, d_ssm, norm_w):
    bsz, l, _ = z.shape
    xbc, conv_new = causal_conv(xbc, conv_st, conv_w, conv_b)
    xbc = jax.nn.silu(xbc)
    nbc = SSM_GROUPS * SSM_STATE
    xs = xbc[..., :D_SSM].reshape(bsz, l, SSM_HEADS, SSM_HEAD_DIM)
    bm = xbc[..., D_SSM:D_SSM + nbc].reshape(bsz, l, SSM_GROUPS, SSM_STATE)
    cm = xbc[..., D_SSM + nbc:].reshape(bsz, l, SSM_GROUPS, SSM_STATE)
    dt = jax.nn.softplus(dt_raw + dt_bias)
    a = -jnp.exp(a_log)
    y, h_new = ssd(xs, dt, a, bm, cm, ssm_st)
    y = (y + d_ssm[:, None] * xs).reshape(bsz, l, D_SSM)
    gy = (y * jax.nn.silu(z)).astype(jnp.float32).reshape(bsz, l, SSM_GROUPS, D_SSM // SSM_GROUPS)
    gy = gy * lax.rsqrt(jnp.mean(gy * gy, -1, keepdims=True) + RMS_EPS)
    y = gy.reshape(bsz, l, D_SSM).astype(z.dtype) * norm_w
    return y, conv_new, h_new


def s5_mixer(u, s0_re, s0_im, a_re, a_im, log_step, b_re, b_im, c_re, c_im, d, w_glu, b_glu):
    bsz, l, _ = u.shape
    step = jnp.exp(log_step)[:, None]
    mag = jnp.exp(a_re * step)
    ang = a_im * step
    ab_re = mag * jnp.cos(ang)
    ab_im = mag * jnp.sin(ang)
    den = a_re * a_re + a_im * a_im
    num_re = ab_re - 1.0
    cf_re = (num_re * a_re + ab_im * a_im) / den
    cf_im = (ab_im * a_re - num_re * a_im) / den
    bb_re = cf_re[..., None] * b_re - cf_im[..., None] * b_im
    bb_im = cf_re[..., None] * b_im + cf_im[..., None] * b_re
    ug = u.reshape(bsz, l, S5_GROUPS, S5_GROUP_CH)
    bu_re = jnp.einsum('gph,blgh->blgp', bb_re, ug)
    bu_im = jnp.einsum('gph,blgh->blgp', bb_im, ug)
    at_re = jnp.broadcast_to(ab_re, (1, l, S5_GROUPS, S5_STATE))
    at_im = jnp.broadcast_to(ab_im, (1, l, S5_GROUPS, S5_STATE))

    def combine(e1, e2):
        ar1, ai1, br1, bi1 = e1
        ar2, ai2, br2, bi2 = e2
        return (ar2 * ar1 - ai2 * ai1, ar2 * ai1 + ai2 * ar1,
                ar2 * br1 - ai2 * bi1 + br2, ar2 * bi1 + ai2 * br1 + bi2)

    acum_re, acum_im, st_re, st_im = lax.associative_scan(combine, (at_re, at_im, bu_re, bu_im), axis=1)
    s_re = st_re + acum_re * s0_re[:, None] - acum_im * s0_im[:, None]
    s_im = st_im + acum_re * s0_im[:, None] + acum_im * s0_re[:, None]
    y = (jnp.einsum('ghp,blgp->blgh', c_re, s_re) - jnp.einsum('ghp,blgp->blgh', c_im, s_im)
         + d.reshape(S5_GROUPS, S5_GROUP_CH) * ug)
    y = jax.nn.gelu(y.reshape(bsz, l, D_S5))
    y = y * jax.nn.sigmoid(y @ w_glu + b_glu)
    return y, s_re[:, -1], s_im[:, -1]


def mem_kv(mem, w_k, w_v):
    b, m, _ = mem.shape
    k = (mem @ w_k).reshape(b, m, MEM_HEADS, MEM_HEAD_DIM)
    v = (mem @ w_v).reshape(b, m, MEM_HEADS, MEM_HEAD_DIM)
    return k, v


def mem_attention(x, k, v, w_q, w_o):
    b, l, _ = x.shape
    q = (x @ w_q).reshape(b, l, MEM_HEADS, MEM_HEAD_DIM)
    s = jnp.einsum('blhd,bmhd->bhlm', q, k).astype(jnp.float32) * (MEM_HEAD_DIM ** -0.5)
    p = jax.nn.softmax(s, axis=-1).astype(v.dtype)
    o = jnp.einsum('bhlm,bmhd->blhd', p, v).reshape(b, l, D_MODEL)
    return o @ w_o


def moe(x2d, w_router, b_router, w_e_gate, w_e_up, w_e_down, w_s_gate, w_s_up, w_s_down):
    t = x2d.shape[0]
    s = jax.nn.sigmoid((x2d @ w_router).astype(jnp.float32))
    sb = s + b_router.astype(jnp.float32)
    per = N_EXPERTS // N_EXPERT_GROUPS
    gscore = jnp.sum(lax.top_k(sb.reshape(t, N_EXPERT_GROUPS, per), 2)[0], axis=-1)
    gidx = lax.top_k(gscore, TOPK_GROUPS)[1]
    gmask = jnp.any(gidx[..., None] == jnp.arange(N_EXPERT_GROUPS), axis=1)
    emask = jnp.repeat(gmask, per, axis=1)
    eidx = lax.top_k(jnp.where(emask, sb, -jnp.inf), TOP_K)[1]
    gates = jnp.take_along_axis(s, eidx, axis=1)
    gates = gates / jnp.sum(gates, -1, keepdims=True) * ROUTE_SCALE
    comb = jnp.sum(jax.nn.one_hot(eidx, N_EXPERTS, dtype=jnp.float32) * gates[..., None], axis=1).astype(x2d.dtype)
    nb = -(-t // MOE_BLOCK)
    pad = nb * MOE_BLOCK - t
    xp = jnp.pad(x2d, ((0, pad), (0, 0))).reshape(nb, MOE_BLOCK, D_MODEL)
    cp = jnp.pad(comb, ((0, pad), (0, 0))).reshape(nb, MOE_BLOCK, N_EXPERTS)

    def block(args):
        xb, cb = args
        hid = jax.nn.silu(jnp.einsum('td,edf->tef', xb, w_e_gate)) * jnp.einsum('td,edf->tef', xb, w_e_up)
        return jnp.einsum('tef,te,efd->td', hid, cb, w_e_down)

    routed = lax.map(block, (xp, cp)).reshape(nb * MOE_BLOCK, D_MODEL)[:t]
    shared = (jax.nn.silu(x2d @ w_s_gate) * (x2d @ w_s_up)) @ w_s_down
    return routed + shared


def layer(x, mem_k, mem_v, conv_st, ssm_st, s5_re, s5_im, w):
    bsz, l, _ = x.shape
    proj = x @ w['w_in']
    o1 = D_SSM
    o2 = o1 + D_CONV
    o3 = o2 + SSM_HEADS
    z, xbc, dt_raw, u = proj[..., :o1], proj[..., o1:o2], proj[..., o2:o3], proj[..., o3:]
    ym, conv_new, ssm_new = mamba_mixer(z, xbc, dt_raw, conv_st, ssm_st, w['conv_w'], w['conv_b'],
                                        w['dt_bias'], w['a_log'], w['d_ssm'], w['norm_ssm'])
    ys, s5_re_new, s5_im_new = s5_mixer(u, s5_re, s5_im, w['s5_a_re'], w['s5_a_im'], w['s5_log_step'],
                                        w['s5_b_re'], w['s5_b_im'], w['s5_c_re'], w['s5_c_im'],
                                        w['s5_d'], w['w_glu'], w['b_glu'])
    mix = jnp.concatenate([ym, ys], axis=-1) @ w['w_mix_out']
    x = layer_norm(ALPHA * x + mix, w['ln1_g'], w['ln1_b'])
    x = layer_norm(ALPHA * x + mem_attention(x, mem_k, mem_v, w['w_q'], w['w_o']), w['ln2_g'], w['ln2_b'])
    f = moe(x.reshape(bsz * l, D_MODEL), w['w_router'], w['b_router'], w['w_e_gate'], w['w_e_up'],
            w['w_e_down'], w['w_s_gate'], w['w_s_up'], w['w_s_down']).reshape(bsz, l, D_MODEL)
    x = layer_norm(ALPHA * x + f, w['ln3_g'], w['ln3_b'])
    return x, conv_new, ssm_new, s5_re_new, s5_im_new


def setup_inputs(seed: int = 0) -> dict:
    key = jax.random.key(seed)
    ks = list(jax.random.split(key, 64))
    cnt = [0]

    def nk():
        cnt[0] += 1
        return ks[cnt[0] - 1]

    def nrm(shape, scale):
        return scale * jax.random.normal(nk(), shape, jnp.float32)

    L = DEPTH
    f32 = jnp.float32
    x_prompt = nrm((BATCH, SEQ, D_MODEL), 1.0)
    x_sample = nrm((DEC_BATCH, DEC_SEQ, D_MODEL), 1.0)
    mem_prompt = nrm((BATCH, MEM_TOKENS, D_MODEL), 1.0)
    state_conv = nrm((L, DEC_BATCH, CONV_W - 1, D_CONV), 1.0)
    state_ssm = nrm((L, DEC_BATCH, SSM_HEADS, SSM_HEAD_DIM, SSM_STATE), 0.1)
    state_s5_re = nrm((L, DEC_BATCH, S5_GROUPS, S5_STATE), 0.1)
    state_s5_im = nrm((L, DEC_BATCH, S5_GROUPS, S5_STATE), 0.1)
    cache_mem_k = nrm((L, DEC_BATCH, MEM_TOKENS, MEM_HEADS, MEM_HEAD_DIM), 1.0)
    cache_mem_v = nrm((L, DEC_BATCH, MEM_TOKENS, MEM_HEADS, MEM_HEAD_DIM), 1.0)
    w_in = nrm((L, D_MODEL, D_IN_PROJ), D_MODEL ** -0.5)
    conv_w = nrm((L, CONV_W, D_CONV), 0.5)
    conv_b = nrm((L, D_CONV), 0.02)
    dt0 = jnp.exp(jax.random.uniform(nk(), (L, SSM_HEADS), f32, math.log(1e-3), math.log(1e-1)))
    dt_bias = dt0 + jnp.log(-jnp.expm1(-dt0))
    a_log = jnp.log(jax.random.uniform(nk(), (L, SSM_HEADS), f32, 1.0, 16.0))
    d_ssm = 1.0 + nrm((L, SSM_HEADS), 0.1)
    norm_ssm = 1.0 + nrm((L, D_SSM), 0.02)
    n_idx = jnp.arange(S5_STATE, dtype=f32)
    s5_a_re = -0.5 + nrm((L, S5_GROUPS, S5_STATE), 0.01)
    s5_a_im = jnp.pi * n_idx + nrm((L, S5_GROUPS, S5_STATE), 0.01)
    s5_log_step = jax.random.uniform(nk(), (L, S5_GROUPS), f32, math.log(1e-3), math.log(1e-1))
    s5_b_re = nrm((L, S5_GROUPS, S5_STATE, S5_GROUP_CH), (2.0 * S5_GROUP_CH) ** -0.5)
    s5_b_im = nrm((L, S5_GROUPS, S5_STATE, S5_GROUP_CH), (2.0 * S5_GROUP_CH) ** -0.5)
    s5_c_re = nrm((L, S5_GROUPS, S5_GROUP_CH, S5_STATE), (2.0 * S5_STATE) ** -0.5)
    s5_c_im = nrm((L, S5_GROUPS, S5_GROUP_CH, S5_STATE), (2.0 * S5_STATE) ** -0.5)
    s5_d = nrm((L, D_S5), 1.0)
    w_glu = nrm((L, D_S5, D_S5), D_S5 ** -0.5)
    b_glu = nrm((L, D_S5), 0.02)
    w_mix_out = nrm((L, D_MIX, D_MODEL), BETA * D_MIX ** -0.5)
    ln1_g = 1.0 + nrm((L, D_MODEL), 0.02)
    ln1_b = nrm((L, D_MODEL), 0.02)
    w_q = nrm((L, D_MODEL, D_MODEL), D_MODEL ** -0.5)
    w_k = nrm((L, D_MODEL, D_MODEL), D_MODEL ** -0.5)
    w_v = nrm((L, D_MODEL, D_MODEL), BETA * D_MODEL ** -0.5)
    w_o = nrm((L, D_MODEL, D_MODEL), BETA * D_MODEL ** -0.5)
    ln2_g = 1.0 + nrm((L, D_MODEL), 0.02)
    ln2_b = nrm((L, D_MODEL), 0.02)
    w_router = nrm((L, D_MODEL, N_EXPERTS), D_MODEL ** -0.5)
    b_router = nrm((L, N_EXPERTS), 0.01)
    w_e_gate = nrm((L, N_EXPERTS, D_MODEL, D_EXPERT), D_MODEL ** -0.5)
    w_e_up = nrm((L, N_EXPERTS, D_MODEL, D_EXPERT), D_MODEL ** -0.5)
    w_e_down = nrm((L, N_EXPERTS, D_EXPERT, D_MODEL), BETA * D_EXPERT ** -0.5)
    w_s_gate = nrm((L, D_MODEL, D_SHARED), D_MODEL ** -0.5)
    w_s_up = nrm((L, D_MODEL, D_SHARED), D_MODEL ** -0.5)
    w_s_down = nrm((L, D_SHARED, D_MODEL), BETA * D_SHARED ** -0.5)
    ln3_g = 1.0 + nrm((L, D_MODEL), 0.02)
    ln3_b = nrm((L, D_MODEL), 0.02)
    return {'x_prompt': x_prompt, 'x_sample': x_sample, 'mem_prompt': mem_prompt,
            'state_conv': state_conv, 'state_ssm': state_ssm, 'state_s5_re': state_s5_re,
            'state_s5_im': state_s5_im, 'cache_mem_k': cache_mem_k, 'cache_mem_v': cache_mem_v,
            'w_in': w_in, 'conv_w': conv_w, 'conv_b': conv_b, 'dt_bias': dt_bias, 'a_log': a_log,
            'd_ssm': d_ssm, 'norm_ssm': norm_ssm, 's5_a_re': s5_a_re, 's5_a_im': s5_a_im,
            's5_log_step': s5_log_step, 's5_b_re': s5_b_re, 's5_b_im': s5_b_im, 's5_c_re': s5_c_re,
            's5_c_im': s5_c_im, 's5_d': s5_d, 'w_glu': w_glu, 'b_glu': b_glu, 'w_mix_out': w_mix_out,
            'ln1_g': ln1_g, 'ln1_b': ln1_b, 'w_q': w_q, 'w_k': w_k, 'w_v': w_v, 'w_o': w_o,
            'ln2_g': ln2_g, 'ln2_b': ln2_b, 'w_router': w_router, 'b_router': b_router,
            'w_e_gate': w_e_gate, 'w_e_up': w_e_up, 'w_e_down': w_e_down, 'w_s_gate': w_s_gate,
            'w_s_up': w_s_up, 'w_s_down': w_s_down, 'ln3_g': ln3_g, 'ln3_b': ln3_b}


def reference(x_prompt, x_sample, mem_prompt, state_conv, state_ssm, state_s5_re, state_s5_im,
              cache_mem_k, cache_mem_v, w_in, conv_w, conv_b, dt_bias, a_log, d_ssm, norm_ssm,
              s5_a_re, s5_a_im, s5_log_step, s5_b_re, s5_b_im, s5_c_re, s5_c_im, s5_d, w_glu, b_glu,
              w_mix_out, ln1_g, ln1_b, w_q, w_k, w_v, w_o, ln2_g, ln2_b, w_router, b_router,
              w_e_gate, w_e_up, w_e_down, w_s_gate, w_s_up, w_s_down, ln3_g, ln3_b):
    yp = x_prompt
    ys = x_sample
    dt_ = x_prompt.dtype
    conv_p, ssm_p, s5r_p, s5i_p, mk_p, mv_p = [], [], [], [], [], []
    conv_s, ssm_s, s5r_s, s5i_s = [], [], [], []
    for l in range(DEPTH):
        w = {'w_in': w_in[l], 'conv_w': conv_w[l], 'conv_b': conv_b[l], 'dt_bias': dt_bias[l],
             'a_log': a_log[l], 'd_ssm': d_ssm[l], 'norm_ssm': norm_ssm[l], 's5_a_re': s5_a_re[l],
             's5_a_im': s5_a_im[l], 's5_log_step': s5_log_step[l], 's5_b_re': s5_b_re[l],
             's5_b_im': s5_b_im[l], 's5_c_re': s5_c_re[l], 's5_c_im': s5_c_im[l], 's5_d': s5_d[l],
             'w_glu': w_glu[l], 'b_glu': b_glu[l], 'w_mix_out': w_mix_out[l], 'ln1_g': ln1_g[l],
             'ln1_b': ln1_b[l], 'w_q': w_q[l], 'w_o': w_o[l], 'ln2_g': ln2_g[l], 'ln2_b': ln2_b[l],
             'w_router': w_router[l], 'b_router': b_router[l], 'w_e_gate': w_e_gate[l],
             'w_e_up': w_e_up[l], 'w_e_down': w_e_down[l], 'w_s_gate': w_s_gate[l],
             'w_s_up': w_s_up[l], 'w_s_down': w_s_down[l], 'ln3_g': ln3_g[l], 'ln3_b': ln3_b[l]}
        mk, mv = mem_kv(mem_prompt, w_k[l], w_v[l])
        b = yp.shape[0]
        yp, c_new, h_new, r_new, i_new = layer(
            yp, mk, mv, jnp.zeros((b, CONV_W - 1, D_CONV), dt_),
            jnp.zeros((b, SSM_HEADS, SSM_HEAD_DIM, SSM_STATE), dt_),
            jnp.zeros((b, S5_GROUPS, S5_STATE), dt_), jnp.zeros((b, S5_GROUPS, S5_STATE), dt_), w)
        conv_p.append(c_new)
        ssm_p.append(h_new)
        s5r_p.append(r_new)
        s5i_p.append(i_new)
        mk_p.append(mk)
        mv_p.append(mv)
        ys, c_new, h_new, r_new, i_new = layer(
            ys, cache_mem_k[l], cache_mem_v[l], state_conv[l], state_ssm[l],
            state_s5_re[l], state_s5_im[l], w)
        conv_s.append(c_new)
        ssm_s.append(h_new)
        s5r_s.append(r_new)
        s5i_s.append(i_new)
    return (yp, ys, jnp.stack(conv_p), jnp.stack(ssm_p), jnp.stack(s5r_p), jnp.stack(s5i_p),
            jnp.stack(mk_p), jnp.stack(mv_p), jnp.stack(conv_s), jnp.stack(ssm_s),
            jnp.stack(s5r_s), jnp.stack(s5i_s))
```

```python
import functools
import math

import jax
import jax.numpy as jnp
from jax import lax
from jax.experimental import pallas as pl
from jax.experimental.pallas import tpu as pltpu

F32 = jnp.float32
BF16 = jnp.bfloat16

D_MODEL = 1024
D_SSM = 1024
SSM_HEAD_DIM = 64
SSM_HEADS = D_SSM // SSM_HEAD_DIM
SSM_GROUPS = 2
SSM_STATE = 128
CONV_W = 4
D_CONV = D_SSM + 2 * SSM_GROUPS * SSM_STATE
D_S5 = 1024
S5_GROUP_CH = 16
S5_GROUPS = D_S5 // S5_GROUP_CH
S5_STATE = 64
MEM_HEADS = 4
MEM_HEAD_DIM = D_MODEL // MEM_HEADS
N_EXPERTS = 64
N_EXPERT_GROUPS = 8
TOPK_GROUPS = 4
TOP_K = 6
ROUTE_SCALE = 2.5
DEPTH = 1
ALPHA = (2.0 * DEPTH) ** 0.25
LN_EPS = 1e-5
RMS_EPS = 1e-5

SSD_CHUNK = 128
SAMPLE_PAD = 8
VMEM_LIMIT = 48 * 1024 * 1024


def _cparams(*sem):
    return pltpu.CompilerParams(dimension_semantics=sem, vmem_limit_bytes=VMEM_LIMIT)


def _dot(a, b):
    return jnp.dot(a, b, preferred_element_type=F32)


def _dot_nt(a, b):
    return lax.dot_general(a, b, (((1,), (1,)), ((), ())), preferred_element_type=F32)


def _dot_tn(a, b):
    return lax.dot_general(a, b, (((0,), (0,)), ((), ())), preferred_element_type=F32)


def _split(v):
    hi = v.astype(BF16)
    lo = (v - hi.astype(F32)).astype(BF16)
    return hi, lo


def _silu(x):
    return x * (1.0 / (1.0 + jnp.exp(-x)))


def _sigmoid(x):
    return 1.0 / (1.0 + jnp.exp(-x))


def _softplus(x):
    return jnp.maximum(x, 0.0) + jnp.log(1.0 + jnp.exp(-jnp.abs(x)))


def _layer_norm(x, g, b):
    mu = jnp.mean(x, axis=-1, keepdims=True)
    xc = x - mu
    var = jnp.mean(xc * xc, axis=-1, keepdims=True)
    return xc * lax.rsqrt(var + LN_EPS) * g + b


def _in_proj_kernel(x_ref, wz_ref, wx_ref, wd_ref, wu_ref, z_ref, xbc_ref, dt_ref, u_ref):
    x = x_ref[...].astype(BF16)
    z_ref[...] = _dot(x, wz_ref[...]).astype(z_ref.dtype)
    xbc_ref[...] = _dot(x, wx_ref[...])
    dt_ref[...] = _dot(x, wd_ref[...])
    u_ref[...] = _dot(x, wu_ref[...]).astype(u_ref.dtype)


def _in_proj(x2d, wz, wx, wd, wu, tm):
    t = x2d.shape[0]
    full = lambda i: (0, 0)
    row = lambda i: (i, 0)
    return pl.pallas_call(
        _in_proj_kernel,
        grid=(t // tm,),
        in_specs=[pl.BlockSpec((tm, D_MODEL), row),
                  pl.BlockSpec(wz.shape, full), pl.BlockSpec(wx.shape, full),
                  pl.BlockSpec(wd.shape, full), pl.BlockSpec(wu.shape, full)],
        out_specs=[pl.BlockSpec((tm, D_SSM), row), pl.BlockSpec((tm, D_CONV), row),
                   pl.BlockSpec((tm, SSM_HEADS), row), pl.BlockSpec((tm, D_S5), row)],
        out_shape=[jax.ShapeDtypeStruct((t, D_SSM), BF16), jax.ShapeDtypeStruct((t, D_CONV), F32),
                   jax.ShapeDtypeStruct((t, SSM_HEADS), F32), jax.ShapeDtypeStruct((t, D_S5), BF16)],
        compiler_params=_cparams("parallel"),
    )(x2d, wz, wx, wd, wu)


def _ssd_kernel(xbc_ref, z_ref, dt_ref, cst_ref, h0_ref, cw_ref, cb_ref, dtb_ref, alog_ref,
                dskip_ref, nw_ref, y_ref, cnew_ref, hout_ref, h_scr, xp_scr, dt_scr,
                *, q, q_in, l_valid):
    c = pl.program_id(1)
    halo = CONV_W - 1
    base = 8

    @pl.when(c == 0)
    def _():
        h_scr[...] = h0_ref[0]
        xp_scr[...] = jnp.zeros_like(xp_scr)
        dt_scr[...] = jnp.zeros_like(dt_scr)
        xp_scr[base - halo:base, :] = cst_ref[0]

    xp_scr[base:base + q_in, :] = xbc_ref[0]
    dt_scr[0:q_in, :] = dt_ref[0]

    w = cw_ref[...]
    acc = cb_ref[...] + w[0:1] * xp_scr[pl.ds(base - halo, q), :]
    for k in range(1, CONV_W):
        acc = acc + w[k:k + 1] * xp_scr[pl.ds(base - halo + k, q), :]
    tail = xp_scr[base + l_valid - halo:base + l_valid, :]
    cnew_ref[0] = tail
    xp_scr[base - halo:base, :] = tail

    xa = _silu(acc)
    xs = xa[:, :D_SSM]
    nbc = SSM_GROUPS * SSM_STATE
    bm = xa[:, D_SSM:D_SSM + nbc].astype(BF16)
    cm = xa[:, D_SSM + nbc:].astype(BF16)

    row = lax.broadcasted_iota(jnp.int32, (q, q), 0)
    col = lax.broadcasted_iota(jnp.int32, (q, q), 1)
    causal = row >= col
    tri = causal.astype(BF16)

    dt = _softplus(dt_scr[...] + dtb_ref[...])
    if l_valid < q:
        dt = jnp.where(lax.broadcasted_iota(jnp.int32, dt.shape, 0) < l_valid, dt, 0.0)
    d_a = dt * (-jnp.exp(alog_ref[...]))
    a_hi, a_lo = _split(d_a)
    acum = _dot(tri, a_hi) + _dot(tri, a_lo)
    eye = (lax.broadcasted_iota(jnp.int32, (SSM_HEADS, SSM_HEADS), 0)
           == lax.broadcasted_iota(jnp.int32, (SSM_HEADS, SSM_HEADS), 1)).astype(BF16)
    c_hi, c_lo = _split(acum)
    acum_t = _dot_nt(eye, c_hi) + _dot_nt(eye, c_lo)

    lane_head = lax.broadcasted_iota(jnp.int32, (SSM_HEADS, D_SSM), 1) // SSM_HEAD_DIM
    expand = (lane_head == lax.broadcasted_iota(jnp.int32, (SSM_HEADS, D_SSM), 0)).astype(BF16)

    def ex(v):
        hi, lo = _split(v)
        return _dot(hi, expand) + _dot(lo, expand)

    acum_last = acum[q - 1:q, :]
    xdt = xs * ex(dt)
    xdt_b = xdt.astype(BF16)
    xw_b = (xdt * ex(jnp.exp(acum_last - acum))).astype(BF16)
    dec_in = ex(jnp.exp(acum))
    dec_chunk = ex(jnp.broadcast_to(jnp.exp(acum_last), (8, SSM_HEADS)))[0:1]

    heads_per_group = SSM_HEADS // SSM_GROUPS
    gw = D_SSM // SSM_GROUPS
    h_old = h_scr[...]
    h_old_b = h_old.astype(BF16)
    lane2 = lax.broadcasted_iota(jnp.int32, (q, 2 * SSM_HEAD_DIM), 1)
    y_parts = []
    new_parts = []
    for g in range(SSM_GROUPS):
        bg = bm[:, g * SSM_STATE:(g + 1) * SSM_STATE]
        cg = cm[:, g * SSM_STATE:(g + 1) * SSM_STATE]
        cb = _dot_nt(cg, bg)
        y_off = _dot(cg, h_old_b[:, g * gw:(g + 1) * gw])
        new_parts.append(_dot_tn(bg, xw_b[:, g * gw:(g + 1) * gw]))
        pair_out = []
        for pr in range(heads_per_group // 2):
            lo_l = g * gw + pr * 2 * SSM_HEAD_DIM
            xp2 = xdt_b[:, lo_l:lo_l + 2 * SSM_HEAD_DIM]
            accp = None
            for sub in range(2):
                h = g * heads_per_group + pr * 2 + sub
                seg = acum[:, h:h + 1] - acum_t[h:h + 1, :]
                lmat = jnp.exp(jnp.where(causal, seg, -jnp.inf))
                gmat = (cb * lmat).astype(BF16)
                in_head = (lane2 // SSM_HEAD_DIM) == sub
                rhs = jnp.where(in_head, xp2, jnp.zeros_like(xp2))
                part = _dot(gmat, rhs)
                accp = part if accp is None else accp + part
            pair_out.append(accp)
        y_diag = jnp.concatenate(pair_out, axis=1)
        y_parts.append(y_diag + y_off * dec_in[:, g * gw:(g + 1) * gw])
    y = jnp.concatenate(y_parts, axis=1) + dskip_ref[...] * xs
    h_new = h_old * dec_chunk + jnp.concatenate(new_parts, axis=1)
    h_scr[...] = h_new
    hout_ref[0] = h_new

    gy = y[0:q_in] * _silu(z_ref[0].astype(F32))
    outs = []
    for g in range(SSM_GROUPS):
        sg = gy[:, g * gw:(g + 1) * gw]
        ms = jnp.mean(sg * sg, axis=-1, keepdims=True)
        outs.append(sg * lax.rsqrt(ms + RMS_EPS))
    y_ref[0] = (jnp.concatenate(outs, axis=1) * nw_ref[...]).astype(y_ref.dtype)


def _ssd(xbc, z, dt, conv_st, h0_t, conv_w, conv_b, dt_bias, a_log, d_skip_x, norm_w, q_in, l_valid):
    b, l, _ = xbc.shape
    q = SSD_CHUNK
    nc = l // q_in
    kern = functools.partial(_ssd_kernel, q=q, q_in=q_in, l_valid=l_valid)
    seq = lambda i, c: (i, c, 0)
    per_b = lambda i, c: (i, 0, 0)
    full = lambda i, c: (0, 0)
    return pl.pallas_call(
        kern,
        grid=(b, nc),
        in_specs=[pl.BlockSpec((1, q_in, D_CONV), seq), pl.BlockSpec((1, q_in, D_SSM), seq),
                  pl.BlockSpec((1, q_in, SSM_HEADS), seq),
                  pl.BlockSpec((1, CONV_W - 1, D_CONV), per_b),
                  pl.BlockSpec((1, SSM_STATE, D_SSM), per_b),
                  pl.BlockSpec(conv_w.shape, full), pl.BlockSpec(conv_b.shape, full),
                  pl.BlockSpec(dt_bias.shape, full), pl.BlockSpec(a_log.shape, full),
                  pl.BlockSpec(d_skip_x.shape, full), pl.BlockSpec(norm_w.shape, full)],
        out_specs=[pl.BlockSpec((1, q_in, D_SSM), seq),
                   pl.BlockSpec((1, CONV_W - 1, D_CONV), per_b),
                   pl.BlockSpec((1, SSM_STATE, D_SSM), per_b)],
        out_shape=[jax.ShapeDtypeStruct((b, l, D_SSM), BF16),
                   jax.ShapeDtypeStruct((b, CONV_W - 1, D_CONV), F32),
                   jax.ShapeDtypeStruct((b, SSM_STATE, D_SSM), F32)],
        scratch_shapes=[pltpu.VMEM((SSM_STATE, D_SSM), F32),
                        pltpu.VMEM((q + 8, D_CONV), F32),
                        pltpu.VMEM((q, SSM_HEADS), F32)],
        compiler_params=_cparams("parallel", "arbitrary"),
    )(xbc, z, dt, conv_st, h0_t, conv_w, conv_b, dt_bias, a_log, d_skip_x, norm_w)


def _s5_tables(a_re, a_im, log_step, b_re, b_im, c_re, c_im, d, qs):
    g, p = a_re.shape
    h = S5_GROUP_CH
    r = qs * h
    step = jnp.exp(log_step)[:, None]
    k = jnp.arange(qs + 1, dtype=F32)[:, None, None]
    mag = jnp.exp(a_re * step * k)
    ang = a_im * step * k
    lr = mag * jnp.cos(ang)
    li = mag * jnp.sin(ang)
    ab_re, ab_im = lr[1], li[1]
    den = a_re * a_re + a_im * a_im
    num_re = ab_re - 1.0
    cf_re = (num_re * a_re + ab_im * a_im) / den
    cf_im = (ab_im * a_re - num_re * a_im) / den
    bb_re = cf_re[..., None] * b_re - cf_im[..., None] * b_im
    bb_im = cf_re[..., None] * b_im + cf_im[..., None] * b_re
    hp = lax.Precision.HIGHEST
    lb_re = lr[:qs, :, :, None] * bb_re - li[:qs, :, :, None] * bb_im
    lb_im = lr[:qs, :, :, None] * bb_im + li[:qs, :, :, None] * bb_re
    kern = (jnp.einsum('ghp,kgpj->kghj', c_re, lb_re, precision=hp)
            - jnp.einsum('ghp,kgpj->kghj', c_im, lb_im, precision=hp))
    tt = jnp.arange(qs)
    lag = tt[:, None] - tt[None, :]
    m = jnp.where((lag >= 0)[:, :, None, None, None],
                  kern[jnp.clip(lag, 0, qs - 1)], 0.0)
    m = jnp.transpose(m, (2, 0, 3, 1, 4)).reshape(g, r, r)
    rev_re = lr[:qs][::-1]
    rev_im = li[:qs][::-1]
    bp_re = rev_re[..., None] * bb_re - rev_im[..., None] * bb_im
    bp_im = rev_re[..., None] * bb_im + rev_im[..., None] * bb_re
    bp_re = jnp.transpose(bp_re, (1, 2, 0, 3)).reshape(g, p, r)
    bp_im = jnp.transpose(bp_im, (1, 2, 0, 3)).reshape(g, p, r)
    fr = lr[1:, :, None, :]
    fi = li[1:, :, None, :]
    cy_re = c_re[None] * fr - c_im[None] * fi
    cy_im = -(c_re[None] * fi + c_im[None] * fr)
    cy_re = jnp.transpose(cy_re, (1, 0, 2, 3)).reshape(g, r, p)
    cy_im = jnp.transpose(cy_im, (1, 0, 2, 3)).reshape(g, r, p)

    np_ = g // 2
    z_pr = jnp.zeros((np_, p, r), F32)
    e = lambda a: a.reshape(np_, 2, *a.shape[1:])
    bre, bim = e(bp_re), e(bp_im)
    bp = jnp.concatenate([
        jnp.concatenate([bre[:, 0], z_pr], axis=2), jnp.concatenate([z_pr, bre[:, 1]], axis=2),
        jnp.concatenate([bim[:, 0], z_pr], axis=2), jnp.concatenate([z_pr, bim[:, 1]], axis=2)], axis=1)
    z_rp = jnp.zeros((np_, r, p), F32)
    cre, cim = e(cy_re), e(cy_im)
    cp = jnp.concatenate([
        jnp.concatenate([cre[:, 0], z_rp, cim[:, 0], z_rp], axis=2),
        jnp.concatenate([z_rp, cre[:, 1], z_rp, cim[:, 1]], axis=2)], axis=1)
    lam_q = jnp.stack([lr[qs].reshape(np_, 2 * p), li[qs].reshape(np_, 2 * p)], axis=1)
    dcol = jnp.broadcast_to(d.reshape(np_, 2, 1, h), (np_, 2, qs, h)).reshape(np_, 2 * r, 1)
    bp_t = jnp.transpose(bp, (0, 2, 1))
    return e(m).astype(BF16), bp_t.astype(BF16), cp.astype(BF16), lam_q, dcol


def _gelu_tanh(x):
    return 0.5 * x * (1.0 + jnp.tanh(math.sqrt(2.0 / math.pi) * (x + 0.044715 * (x * x * x))))


def _s5_kernel(u_ref, m_ref, bp_ref, cp_ref, lam_ref, d_ref, s0_ref, y_ref, sout_ref,
               loc_re, loc_im, st_re, st_im, *, nb, nc):
    r = m_ref.shape[2]
    pw = lam_ref.shape[2]
    u = u_ref[0]
    loc_re[...] = _dot_tn(u, bp_ref[0, :, :pw])
    loc_im[...] = _dot_tn(u, bp_ref[0, :, pw:])
    ar = lam_ref[0, 0:1, :]
    ai = lam_ref[0, 1:2, :]
    sr = s0_ref[0, 0]
    si = s0_ref[0, 1]
    for c in range(nc):
        rows = pl.ds(c, nb, stride=nc) if nc > 1 else pl.ds(0, nb)
        st_re[rows, :] = sr
        st_im[rows, :] = si
        sr, si = (ar * sr - ai * si + loc_re[rows, :], ar * si + ai * sr + loc_im[rows, :])
    sout_ref[0, 0] = sr
    sout_ref[0, 1] = si
    y_off = (_dot_nt(cp_ref[0, :, :pw], st_re[...].astype(BF16))
             + _dot_nt(cp_ref[0, :, pw:], st_im[...].astype(BF16)))
    y_loc = jnp.concatenate([_dot(m_ref[0, 0], u[:r]), _dot(m_ref[0, 1], u[r:])], axis=0)
    y = y_loc + y_off + d_ref[0] * u.astype(F32)
    y_ref[0] = _gelu_tanh(y).astype(y_ref.dtype)


def _s5(u, s0_re, s0_im, tables, qs):
    m, bp, cp, lam_q, dcol = tables
    b, l, _ = u.shape
    nc = l // qs
    np_ = S5_GROUPS // 2
    r = qs * S5_GROUP_CH
    nbc = b * nc
    pw = 2 * S5_STATE
    ut = jnp.transpose(u.reshape(b, nc, qs, np_, 2, S5_GROUP_CH), (3, 4, 2, 5, 0, 1)).reshape(np_, 2 * r, nbc)
    pack = lambda s: jnp.transpose(s.reshape(b, np_, pw), (1, 0, 2))
    s0 = jnp.stack([pack(s0_re), pack(s0_im)], axis=1)
    kern = functools.partial(_s5_kernel, nb=b, nc=nc)
    blk = lambda *shape: pl.BlockSpec((1,) + shape, lambda i: (i,) + (0,) * len(shape))
    yt, sout = pl.pallas_call(
        kern,
        grid=(np_,),
        in_specs=[blk(2 * r, nbc), blk(2, r, r), blk(2 * r, 2 * pw), blk(2 * r, 2 * pw),
                  blk(2, pw), blk(2 * r, 1), blk(2, b, pw)],
        out_specs=[blk(2 * r, nbc), blk(2, b, pw)],
        out_shape=[jax.ShapeDtypeStruct((np_, 2 * r, nbc), BF16),
                   jax.ShapeDtypeStruct((np_, 2, b, pw), F32)],
        scratch_shapes=[pltpu.VMEM((nbc, pw), F32)] * 4,
        compiler_params=_cparams("parallel"),
    )(ut, m, bp, cp, lam_q, dcol, s0)
    y = jnp.transpose(yt.reshape(np_, 2, qs, S5_GROUP_CH, b, nc), (4, 5, 2, 0, 1, 3)).reshape(b, l, D_S5)
    unpack = lambda s: jnp.transpose(s, (1, 0, 2)).reshape(b, S5_GROUPS, S5_STATE)
    return y, unpack(sout[:, 0]), unpack(sout[:, 1])


def _mix_kernel(ym_ref, yg_ref, x_ref, wglu_ref, bglu_ref, wmo_ref, g_ref, b_ref, o_ref):
    yg = yg_ref[...]
    gate = _sigmoid(_dot(yg, wglu_ref[...]) + bglu_ref[...])
    ys = (yg.astype(F32) * gate).astype(BF16)
    mix = _dot(ym_ref[...], wmo_ref[0:D_SSM, :]) + _dot(ys, wmo_ref[D_SSM:, :])
    o_ref[...] = _layer_norm(ALPHA * x_ref[...] + mix, g_ref[...], b_ref[...])


def _mix(ym, yg, x2d, wglu, bglu, wmo, g, b, tm):
    t = x2d.shape[0]
    full = lambda i: (0, 0)
    row = lambda i: (i, 0)
    return pl.pallas_call(
        _mix_kernel,
        grid=(t // tm,),
        in_specs=[pl.BlockSpec((tm, D_SSM), row), pl.BlockSpec((tm, D_S5), row),
                  pl.BlockSpec((tm, D_MODEL), row),
                  pl.BlockSpec(wglu.shape, full), pl.BlockSpec(bglu.shape, full),
                  pl.BlockSpec(wmo.shape, full), pl.BlockSpec(g.shape, full), pl.BlockSpec(b.shape, full)],
        out_specs=pl.BlockSpec((tm, D_MODEL), row),
        out_shape=jax.ShapeDtypeStruct((t, D_MODEL), F32),
        compiler_params=_cparams("parallel"),
    )(ym, yg, x2d, wglu, bglu, wmo, g, b)


def _memkv_kernel(m_ref, wk_ref, wv_ref, k_ref, v_ref):
    m = m_ref[...].astype(BF16)
    k_ref[...] = _dot(m, wk_ref[...])
    v_ref[...] = _dot(m, wv_ref[...])


def _memkv(mem2d, wk, wv, tm):
    t = mem2d.shape[0]
    full = lambda i: (0, 0)
    row = lambda i: (i, 0)
    return pl.pallas_call(
        _memkv_kernel,
        grid=(t // tm,),
        in_specs=[pl.BlockSpec((tm, D_MODEL), row), pl.BlockSpec(wk.shape, full), pl.BlockSpec(wv.shape, full)],
        out_specs=[pl.BlockSpec((tm, D_MODEL), row), pl.BlockSpec((tm, D_MODEL), row)],
        out_shape=[jax.ShapeDtypeStruct((t, D_MODEL), F32), jax.ShapeDtypeStruct((t, D_MODEL), F32)],
        compiler_params=_cparams("parallel"),
    )(mem2d, wk, wv)


def _attn_kernel(x_ref, k_ref, v_ref, wq_ref, wo_ref, g_ref, b_ref, o_ref):
    x = x_ref[0]
    q = _dot(x.astype(BF16), wq_ref[...]).astype(BF16)
    k = k_ref[0].astype(BF16)
    v = v_ref[0].astype(BF16)
    scale = MEM_HEAD_DIM ** -0.5
    outs = []
    for h in range(MEM_HEADS):
        sl = slice(h * MEM_HEAD_DIM, (h + 1) * MEM_HEAD_DIM)
        s = _dot_nt(q[:, sl], k[:, sl]) * scale
        s = s - jnp.max(s, axis=-1, keepdims=True)
        p = jnp.exp(s)
        p = p / jnp.sum(p, axis=-1, keepdims=True)
        outs.append(_dot(p.astype(BF16), v[:, sl]))
    o = jnp.concatenate(outs, axis=1).astype(BF16)
    att = _dot(o, wo_ref[...])
    o_ref[0] = _layer_norm(ALPHA * x + att, g_ref[...], b_ref[...])


def _attn(x3d, k, v, wq, wo, g, b, tq):
    bsz, l, _ = x3d.shape
    m = k.shape[1]
    full = lambda i, j: (0, 0)
    seq = lambda i, j: (i, j, 0)
    per_b = lambda i, j: (i, 0, 0)
    return pl.pallas_call(
        _attn_kernel,
        grid=(bsz, l // tq),
        in_specs=[pl.BlockSpec((1, tq, D_MODEL), seq),
                  pl.BlockSpec((1, m, D_MODEL), per_b), pl.BlockSpec((1, m, D_MODEL), per_b),
                  pl.BlockSpec(wq.shape, full), pl.BlockSpec(wo.shape, full),
                  pl.BlockSpec(g.shape, full), pl.BlockSpec(b.shape, full)],
        out_specs=pl.BlockSpec((1, tq, D_MODEL), seq),
        out_shape=jax.ShapeDtypeStruct((bsz, l, D_MODEL), F32),
        compiler_params=_cparams("parallel", "parallel"),
    )(x3d, k, v, wq, wo, g, b)


def _router_kernel(x_ref, wh_ref, wl_ref, br_ref, comb_ref):
    x = x_ref[...]
    tm = x.shape[0]
    xh, xl = _split(x)
    wh = wh_ref[...]
    logits = _dot_nt(wh, xh) + _dot_nt(wh, xl) + _dot_nt(wl_ref[...], xh)
    s = _sigmoid(logits)
    sb = s + br_ref[...]
    per = N_EXPERTS // N_EXPERT_GROUPS
    neg = -jnp.inf

    sb3 = sb.reshape(N_EXPERT_GROUPS, per, tm)
    mem = lax.broadcasted_iota(jnp.int32, sb3.shape, 1)
    m1 = jnp.max(sb3, axis=1, keepdims=True)
    first = jnp.min(jnp.where(sb3 == m1, mem, per), axis=1, keepdims=True)
    m2 = jnp.max(jnp.where(mem == first, neg, sb3), axis=1, keepdims=True)
    gs = (m1 + m2).reshape(N_EXPERT_GROUPS, tm)

    def rank_rows(v):
        n = v.shape[0]
        idx = lax.broadcasted_iota(jnp.int32, v.shape, 0)
        rank = jnp.zeros(v.shape, jnp.int32)
        for j in range(n):
            rj = v[j:j + 1, :]
            beats = (rj > v) | ((rj == v) & (idx > j))
            rank = rank + beats.astype(jnp.int32)
        return rank

    gsel = rank_rows(gs) < TOPK_GROUPS
    emask = jnp.broadcast_to(gsel.reshape(N_EXPERT_GROUPS, 1, tm), sb3.shape).reshape(N_EXPERTS, tm)
    sm = jnp.where(emask, sb, neg)
    sel = (rank_rows(sm) < TOP_K) & emask
    gates = jnp.where(sel, s, 0.0)
    denom = jnp.sum(gates, axis=0, keepdims=True)
    comb_t = gates / denom * ROUTE_SCALE
    comb_pad = jnp.concatenate([comb_t, jnp.zeros_like(comb_t)], axis=0)
    comb_ref[...] = comb_pad.T


def _router(x2d, wr_hi_t, wr_lo_t, br_col, tm):
    t = x2d.shape[0]
    full = lambda i: (0, 0)
    row = lambda i: (i, 0)
    return pl.pallas_call(
        _router_kernel,
        grid=(t // tm,),
        in_specs=[pl.BlockSpec((tm, D_MODEL), row), pl.BlockSpec(wr_hi_t.shape, full),
                  pl.BlockSpec(wr_lo_t.shape, full), pl.BlockSpec(br_col.shape, full)],
        out_specs=pl.BlockSpec((tm, 2 * N_EXPERTS), row),
        out_shape=jax.ShapeDtypeStruct((t, 2 * N_EXPERTS), F32),
        compiler_params=_cparams("parallel"),
    )(x2d, wr_hi_t, wr_lo_t, br_col)


def _moe_kernel(x_ref, comb_ref, wg_ref, wu_ref, wd_ref, sg_ref, su_ref, sd_ref, g_ref, b_ref,
                o_ref, acc_scr, xb_scr):
    e = pl.program_id(1)

    @pl.when(e == 0)
    def _():
        xb0 = x_ref[...].astype(BF16)
        xb_scr[...] = xb0
        hs = _silu(_dot(xb0, sg_ref[...])) * _dot(xb0, su_ref[...])
        acc_scr[...] = _dot(hs.astype(BF16), sd_ref[...])

    xb = xb_scr[...]
    hid = _silu(_dot(xb, wg_ref[0].astype(BF16))) * _dot(xb, wu_ref[0].astype(BF16))
    comb = comb_ref[...]
    lane = lax.broadcasted_iota(jnp.int32, comb.shape, 1)
    ce = jnp.sum(jnp.where(lane == e, comb, 0.0), axis=1, keepdims=True)
    acc_scr[...] += _dot((hid * ce).astype(BF16), wd_ref[0].astype(BF16))

    @pl.when(e == pl.num_programs(1) - 1)
    def _():
        o_ref[...] = _layer_norm(ALPHA * x_ref[...] + acc_scr[...], g_ref[...], b_ref[...])


def _moe(x2d, comb, wg, wu, wd, sg, su, sd, g, b, tm):
    t = x2d.shape[0]
    ne, _, de = wg.shape
    full = lambda i, e: (0, 0)
    row = lambda i, e: (i, 0)
    exp = lambda i, e: (e, 0, 0)
    return pl.pallas_call(
        _moe_kernel,
        grid=(t // tm, ne),
        in_specs=[pl.BlockSpec((tm, D_MODEL), row), pl.BlockSpec((tm, comb.shape[1]), row),
                  pl.BlockSpec((1, D_MODEL, de), exp), pl.BlockSpec((1, D_MODEL, de), exp),
                  pl.BlockSpec((1, de, D_MODEL), exp),
                  pl.BlockSpec(sg.shape, full), pl.BlockSpec(su.shape, full), pl.BlockSpec(sd.shape, full),
                  pl.BlockSpec(g.shape, full), pl.BlockSpec(b.shape, full)],
        out_specs=pl.BlockSpec((tm, D_MODEL), row),
        out_shape=jax.ShapeDtypeStruct((t, D_MODEL), F32),
        scratch_shapes=[pltpu.VMEM((tm, D_MODEL), F32), pltpu.VMEM((tm, D_MODEL), BF16)],
        compiler_params=_cparams("parallel", "arbitrary"),
    )(x2d, comb, wg, wu, wd, sg, su, sd, g, b)


def _layer(x3d, mem_k, mem_v, conv_st, ssm_st, s5_re, s5_im, w, *, l_valid, s5_chunk, tm, tq, tm_moe):
    bsz, lp, _ = x3d.shape
    t = bsz * lp
    x2d = x3d.reshape(t, D_MODEL)
    tm = min(tm, t)
    tm_moe = min(tm_moe, t)
    tq = min(tq, lp)
    z, xbc, dt, u = _in_proj(x2d, w['wz'], w['wx'], w['wd'], w['wu'], tm)
    q_in = min(lp, SSD_CHUNK)
    h0_t = jnp.transpose(ssm_st.reshape(bsz, D_SSM, SSM_STATE), (0, 2, 1))
    ym, conv_new, h_t = _ssd(xbc.reshape(bsz, lp, D_CONV), z.reshape(bsz, lp, D_SSM),
                             dt.reshape(bsz, lp, SSM_HEADS), conv_st, h0_t,
                             w['conv_w'], w['conv_b'], w['dt_bias'], w['a_log'], w['d_skip_x'],
                             w['norm_ssm'], q_in, min(l_valid, q_in))
    ssm_new = jnp.transpose(h_t, (0, 2, 1)).reshape(bsz, SSM_HEADS, SSM_HEAD_DIM, SSM_STATE)
    u3 = u.reshape(bsz, lp, D_S5)[:, :l_valid]
    yg, s5_re_new, s5_im_new = _s5(u3, s5_re, s5_im, w['s5_tables'][s5_chunk], s5_chunk)
    if lp > l_valid:
        yg = jnp.pad(yg, ((0, 0), (0, lp - l_valid), (0, 0)))
    x1 = _mix(ym.reshape(t, D_SSM), yg.reshape(t, D_S5), x2d, w['w_glu'], w['b_glu'], w['w_mix_out'],
              w['ln1_g'], w['ln1_b'], tm)
    x2 = _attn(x1.reshape(bsz, lp, D_MODEL), mem_k, mem_v, w['w_q'], w['w_o'], w['ln2_g'], w['ln2_b'], tq)
    x2 = x2.reshape(t, D_MODEL)
    comb = _router(x2, w['wr_hi_t'], w['wr_lo_t'], w['br_col'], tm)
    y = _moe(x2, comb, w['w_e_gate'], w['w_e_up'], w['w_e_down'], w['w_s_gate'], w['w_s_up'],
             w['w_s_down'], w['ln3_g'], w['ln3_b'], tm_moe)
    return y.reshape(bsz, lp, D_MODEL), conv_new, ssm_new, s5_re_new, s5_im_new


def kernel(x_prompt, x_sample, mem_prompt, state_conv, state_ssm, state_s5_re, state_s5_im, cache_mem_k, cache_mem_v, w_in, conv_w, conv_b, dt_bias, a_log, d_ssm, norm_ssm, s5_a_re, s5_a_im, s5_log_step, s5_b_re, s5_b_im, s5_c_re, s5_c_im, s5_d, w_glu, b_glu, w_mix_out, ln1_g, ln1_b, w_q, w_k, w_v, w_o, ln2_g, ln2_b, w_router, b_router, w_e_gate, w_e_up, w_e_down, w_s_gate, w_s_up, w_s_down, ln3_g, ln3_b):
    assert w_in.shape[0] == DEPTH
    bp, lp, _ = x_prompt.shape
    bs, ls, _ = x_sample.shape
    row = lambda a: a.reshape(1, -1)
    bf = lambda a: a.astype(BF16)
    w_in0 = w_in[0]
    o1 = D_SSM
    o2 = o1 + D_CONV
    o3 = o2 + SSM_HEADS
    wr_hi, wr_lo = _split(w_router[0].T)
    s5_args = (s5_a_re[0], s5_a_im[0], s5_log_step[0], s5_b_re[0], s5_b_im[0], s5_c_re[0], s5_c_im[0], s5_d[0])
    s5_chunk_p = math.gcd(lp, 16)
    w = {
        'wz': bf(w_in0[:, :o1]), 'wx': bf(w_in0[:, o1:o2]), 'wd': bf(w_in0[:, o2:o3]), 'wu': bf(w_in0[:, o3:]),
        'conv_w': conv_w[0], 'conv_b': row(conv_b[0]), 'dt_bias': row(dt_bias[0]), 'a_log': row(a_log[0]),
        'd_skip_x': row(jnp.repeat(d_ssm[0], SSM_HEAD_DIM)), 'norm_ssm': row(norm_ssm[0]),
        's5_tables': {q: _s5_tables(*s5_args, q) for q in sorted({s5_chunk_p, ls})},
        'w_glu': bf(w_glu[0]), 'b_glu': row(b_glu[0]), 'w_mix_out': bf(w_mix_out[0]),
        'ln1_g': row(ln1_g[0]), 'ln1_b': row(ln1_b[0]),
        'w_q': bf(w_q[0]), 'w_o': bf(w_o[0]), 'ln2_g': row(ln2_g[0]), 'ln2_b': row(ln2_b[0]),
        'wr_hi_t': wr_hi, 'wr_lo_t': wr_lo, 'br_col': b_router[0].reshape(-1, 1),
        'w_e_gate': w_e_gate[0], 'w_e_up': w_e_up[0], 'w_e_down': w_e_down[0],
        'w_s_gate': bf(w_s_gate[0]), 'w_s_up': bf(w_s_up[0]), 'w_s_down': bf(w_s_down[0]),
        'ln3_g': row(ln3_g[0]), 'ln3_b': row(ln3_b[0]),
    }

    m_tok = mem_prompt.shape[1]
    mk2d, mv2d = _memkv(mem_prompt.reshape(bp * m_tok, D_MODEL), bf(w_k[0]), bf(w_v[0]), 512)
    mk = mk2d.reshape(bp, m_tok, D_MODEL)
    mv = mv2d.reshape(bp, m_tok, D_MODEL)
    zeros = lambda *s: jnp.zeros(s, F32)
    yp, conv_p, ssm_p, s5r_p, s5i_p = _layer(
        x_prompt, mk, mv, zeros(bp, CONV_W - 1, D_CONV), zeros(bp, SSM_HEADS, SSM_HEAD_DIM, SSM_STATE),
        zeros(bp, S5_GROUPS, S5_STATE), zeros(bp, S5_GROUPS, S5_STATE), w,
        l_valid=lp, s5_chunk=s5_chunk_p, tm=512, tq=512, tm_moe=1024)

    xs_pad = jnp.pad(x_sample, ((0, 0), (0, SAMPLE_PAD - ls), (0, 0)))
    ck = cache_mem_k[0].reshape(bs, m_tok, D_MODEL)
    cv = cache_mem_v[0].reshape(bs, m_tok, D_MODEL)
    ys, conv_s, ssm_s, s5r_s, s5i_s = _layer(
        xs_pad, ck, cv, state_conv[0], state_ssm[0], state_s5_re[0], state_s5_im[0], w,
        l_valid=ls, s5_chunk=ls, tm=512, tq=SAMPLE_PAD, tm_moe=1024)
    ys = ys[:, :ls]

    kv_shape = (1, bp, m_tok, MEM_HEADS, MEM_HEAD_DIM)
    return (yp, ys, conv_p[None], ssm_p[None], s5r_p[None], s5i_p[None],
            mk.reshape(kv_shape), mv.reshape(kv_shape),
            conv_s[None], ssm_s[None], s5r_s[None], s5i_s[None])
```

```python
import functools
import math

import jax
import jax.numpy as jnp
from jax import lax
from jax.experimental import pallas as pl
from jax.experimental.pallas import tpu as pltpu

F32 = jnp.float32
BF16 = jnp.bfloat16

D_MODEL = 1024
D_SSM = 1024
SSM_HEAD_DIM = 64
SSM_HEADS = D_SSM // SSM_HEAD_DIM
SSM_GROUPS = 2
SSM_STATE = 128
CONV_W = 4
D_CONV = D_SSM + 2 * SSM_GROUPS * SSM_STATE
D_S5 = 1024
S5_GROUP_CH = 16
S5_GROUPS = D_S5 // S5_GROUP_CH
S5_STATE = 64
MEM_HEADS = 4
MEM_HEAD_DIM = D_MODEL // MEM_HEADS
N_EXPERTS = 64
N_EXPERT_GROUPS = 8
TOPK_GROUPS = 4
TOP_K = 6
ROUTE_SCALE = 2.5
DEPTH = 1
ALPHA = (2.0 * DEPTH) ** 0.25
LN_EPS = 1e-5
RMS_EPS = 1e-5

SSD_CHUNK = 128
SAMPLE_PAD = 8
MOE_TM = 512
MOE_TR = 512
MOE_SEG = 16
VMEM_LIMIT = 48 * 1024 * 1024


def _cparams(*sem):
    return pltpu.CompilerParams(dimension_semantics=sem, vmem_limit_bytes=VMEM_LIMIT)


def _dot(a, b):
    return jnp.dot(a, b, preferred_element_type=F32)


def _dot_nt(a, b):
    return lax.dot_general(a, b, (((1,), (1,)), ((), ())), preferred_element_type=F32)


def _dot_tn(a, b):
    return lax.dot_general(a, b, (((0,), (0,)), ((), ())), preferred_element_type=F32)


def _split(v):
    hi = v.astype(BF16)
    lo = (v - hi.astype(F32)).astype(BF16)
    return hi, lo


def _silu(x):
    return x * (1.0 / (1.0 + jnp.exp(-x)))


def _sigmoid(x):
    return 1.0 / (1.0 + jnp.exp(-x))


def _softplus(x):
    return jnp.maximum(x, 0.0) + jnp.log(1.0 + jnp.exp(-jnp.abs(x)))


def _layer_norm(x, g, b):
    mu = jnp.mean(x, axis=-1, keepdims=True)
    xc = x - mu
    var = jnp.mean(xc * xc, axis=-1, keepdims=True)
    return xc * lax.rsqrt(var + LN_EPS) * g + b


def _in_proj_kernel(x_ref, wz_ref, wx_ref, wd_ref, wu_ref, z_ref, xbc_ref, dt_ref, u_ref):
    x = x_ref[...].astype(BF16)
    z_ref[...] = _dot(x, wz_ref[...]).astype(z_ref.dtype)
    xbc_ref[...] = _dot(x, wx_ref[...])
    dt_ref[...] = _dot(x, wd_ref[...])
    u_ref[...] = _dot(x, wu_ref[...]).astype(u_ref.dtype)


def _in_proj(x2d, wz, wx, wd, wu, tm):
    t = x2d.shape[0]
    full = lambda i: (0, 0)
    row = lambda i: (i, 0)
    return pl.pallas_call(
        _in_proj_kernel,
        grid=(t // tm,),
        in_specs=[pl.BlockSpec((tm, D_MODEL), row),
                  pl.BlockSpec(wz.shape, full), pl.BlockSpec(wx.shape, full),
                  pl.BlockSpec(wd.shape, full), pl.BlockSpec(wu.shape, full)],
        out_specs=[pl.BlockSpec((tm, D_SSM), row), pl.BlockSpec((tm, D_CONV), row),
                   pl.BlockSpec((tm, SSM_HEADS), row), pl.BlockSpec((tm, D_S5), row)],
        out_shape=[jax.ShapeDtypeStruct((t, D_SSM), BF16), jax.ShapeDtypeStruct((t, D_CONV), F32),
                   jax.ShapeDtypeStruct((t, SSM_HEADS), F32), jax.ShapeDtypeStruct((t, D_S5), BF16)],
        compiler_params=_cparams("parallel"),
    )(x2d, wz, wx, wd, wu)


def _ssd_kernel(xbc_ref, z_ref, dt_ref, cst_ref, h0_ref, cw_ref, cb_ref, dtb_ref, alog_ref,
                dskip_ref, nw_ref, y_ref, cnew_ref, hout_ref, h_scr, xp_scr, dt_scr,
                *, q, q_in, l_valid):
    c = pl.program_id(1)
    halo = CONV_W - 1
    base = 8

    @pl.when(c == 0)
    def _():
        h_scr[...] = h0_ref[0]
        xp_scr[...] = jnp.zeros_like(xp_scr)
        dt_scr[...] = jnp.zeros_like(dt_scr)
        xp_scr[base - halo:base, :] = cst_ref[0]

    xp_scr[base:base + q_in, :] = xbc_ref[0]
    dt_scr[0:q_in, :] = dt_ref[0]

    w = cw_ref[...]
    acc = cb_ref[...] + w[0:1] * xp_scr[pl.ds(base - halo, q), :]
    for k in range(1, CONV_W):
        acc = acc + w[k:k + 1] * xp_scr[pl.ds(base - halo + k, q), :]
    tail = xp_scr[base + l_valid - halo:base + l_valid, :]
    cnew_ref[0] = tail
    xp_scr[base - halo:base, :] = tail

    xa = _silu(acc)
    xs = xa[:, :D_SSM]
    nbc = SSM_GROUPS * SSM_STATE
    bm = xa[:, D_SSM:D_SSM + nbc].astype(BF16)
    cm = xa[:, D_SSM + nbc:].astype(BF16)

    row = lax.broadcasted_iota(jnp.int32, (q, q), 0)
    col = lax.broadcasted_iota(jnp.int32, (q, q), 1)
    causal = row >= col
    tri = causal.astype(BF16)

    dt = _softplus(dt_scr[...] + dtb_ref[...])
    if l_valid < q:
        dt = jnp.where(lax.broadcasted_iota(jnp.int32, dt.shape, 0) < l_valid, dt, 0.0)
    d_a = dt * (-jnp.exp(alog_ref[...]))
    a_hi, a_lo = _split(d_a)
    acum = _dot(tri, a_hi) + _dot(tri, a_lo)
    eye = (lax.broadcasted_iota(jnp.int32, (SSM_HEADS, SSM_HEADS), 0)
           == lax.broadcasted_iota(jnp.int32, (SSM_HEADS, SSM_HEADS), 1)).astype(BF16)
    c_hi, c_lo = _split(acum)
    acum_t = _dot_nt(eye, c_hi) + _dot_nt(eye, c_lo)

    lane_head = lax.broadcasted_iota(jnp.int32, (SSM_HEADS, D_SSM), 1) // SSM_HEAD_DIM
    expand = (lane_head == lax.broadcasted_iota(jnp.int32, (SSM_HEADS, D_SSM), 0)).astype(BF16)

    def ex(v):
        hi, lo = _split(v)
        return _dot(hi, expand) + _dot(lo, expand)

    acum_last = acum[q - 1:q, :]
    xdt = xs * ex(dt)
    xdt_b = xdt.astype(BF16)
    xw_b = (xdt * ex(jnp.exp(acum_last - acum))).astype(BF16)
    dec_in = ex(jnp.exp(acum))
    dec_chunk = ex(jnp.broadcast_to(jnp.exp(acum_last), (8, SSM_HEADS)))[0:1]

    heads_per_group = SSM_HEADS // SSM_GROUPS
    gw = D_SSM // SSM_GROUPS
    h_old = h_scr[...]
    h_old_b = h_old.astype(BF16)
    lane2 = lax.broadcasted_iota(jnp.int32, (q, 2 * SSM_HEAD_DIM), 1)
    y_parts = []
    new_parts = []
    for g in range(SSM_GROUPS):
        bg = bm[:, g * SSM_STATE:(g + 1) * SSM_STATE]
        cg = cm[:, g * SSM_STATE:(g + 1) * SSM_STATE]
        cb = _dot_nt(cg, bg)
        y_off = _dot(cg, h_old_b[:, g * gw:(g + 1) * gw])
        new_parts.append(_dot_tn(bg, xw_b[:, g * gw:(g + 1) * gw]))
        pair_out = []
        for pr in range(heads_per_group // 2):
            lo_l = g * gw + pr * 2 * SSM_HEAD_DIM
            xp2 = xdt_b[:, lo_l:lo_l + 2 * SSM_HEAD_DIM]
            accp = None
            for sub in range(2):
                h = g * heads_per_group + pr * 2 + sub
                seg = acum[:, h:h + 1] - acum_t[h:h + 1, :]
                lmat = jnp.exp(jnp.where(causal, seg, -jnp.inf))
                gmat = (cb * lmat).astype(BF16)
                in_head = (lane2 // SSM_HEAD_DIM) == sub
                rhs = jnp.where(in_head, xp2, jnp.zeros_like(xp2))
                part = _dot(gmat, rhs)
                accp = part if accp is None else accp + part
            pair_out.append(accp)
        y_diag = jnp.concatenate(pair_out, axis=1)
        y_parts.append(y_diag + y_off * dec_in[:, g * gw:(g + 1) * gw])
    y = jnp.concatenate(y_parts, axis=1) + dskip_ref[...] * xs
    h_new = h_old * dec_chunk + jnp.concatenate(new_parts, axis=1)
    h_scr[...] = h_new
    hout_ref[0] = h_new

    gy = y[0:q_in] * _silu(z_ref[0].astype(F32))
    outs = []
    for g in range(SSM_GROUPS):
        sg = gy[:, g * gw:(g + 1) * gw]
        ms = jnp.mean(sg * sg, axis=-1, keepdims=True)
        outs.append(sg * lax.rsqrt(ms + RMS_EPS))
    y_ref[0] = (jnp.concatenate(outs, axis=1) * nw_ref[...]).astype(y_ref.dtype)


def _ssd(xbc, z, dt, conv_st, h0_t, conv_w, conv_b, dt_bias, a_log, d_skip_x, norm_w, q_in, l_valid):
    b, l, _ = xbc.shape
    q = SSD_CHUNK
    nc = l // q_in
    kern = functools.partial(_ssd_kernel, q=q, q_in=q_in, l_valid=l_valid)
    seq = lambda i, c: (i, c, 0)
    per_b = lambda i, c: (i, 0, 0)
    full = lambda i, c: (0, 0)
    return pl.pallas_call(
        kern,
        grid=(b, nc),
        in_specs=[pl.BlockSpec((1, q_in, D_CONV), seq), pl.BlockSpec((1, q_in, D_SSM), seq),
                  pl.BlockSpec((1, q_in, SSM_HEADS), seq),
                  pl.BlockSpec((1, CONV_W - 1, D_CONV), per_b),
                  pl.BlockSpec((1, SSM_STATE, D_SSM), per_b),
                  pl.BlockSpec(conv_w.shape, full), pl.BlockSpec(conv_b.shape, full),
                  pl.BlockSpec(dt_bias.shape, full), pl.BlockSpec(a_log.shape, full),
                  pl.BlockSpec(d_skip_x.shape, full), pl.BlockSpec(norm_w.shape, full)],
        out_specs=[pl.BlockSpec((1, q_in, D_SSM), seq),
                   pl.BlockSpec((1, CONV_W - 1, D_CONV), per_b),
                   pl.BlockSpec((1, SSM_STATE, D_SSM), per_b)],
        out_shape=[jax.ShapeDtypeStruct((b, l, D_SSM), BF16),
                   jax.ShapeDtypeStruct((b, CONV_W - 1, D_CONV), F32),
                   jax.ShapeDtypeStruct((b, SSM_STATE, D_SSM), F32)],
        scratch_shapes=[pltpu.VMEM((SSM_STATE, D_SSM), F32),
                        pltpu.VMEM((q + 8, D_CONV), F32),
                        pltpu.VMEM((q, SSM_HEADS), F32)],
        compiler_params=_cparams("parallel", "arbitrary"),
    )(xbc, z, dt, conv_st, h0_t, conv_w, conv_b, dt_bias, a_log, d_skip_x, norm_w)


def _s5_tables(a_re, a_im, log_step, b_re, b_im, c_re, c_im, d, qs):
    g, p = a_re.shape
    h = S5_GROUP_CH
    r = qs * h
    step = jnp.exp(log_step)[:, None]
    k = jnp.arange(qs + 1, dtype=F32)[:, None, None]
    mag = jnp.exp(a_re * step * k)
    ang = a_im * step * k
    lr = mag * jnp.cos(ang)
    li = mag * jnp.sin(ang)
    ab_re, ab_im = lr[1], li[1]
    den = a_re * a_re + a_im * a_im
    num_re = ab_re - 1.0
    cf_re = (num_re * a_re + ab_im * a_im) / den
    cf_im = (ab_im * a_re - num_re * a_im) / den
    bb_re = cf_re[..., None] * b_re - cf_im[..., None] * b_im
    bb_im = cf_re[..., None] * b_im + cf_im[..., None] * b_re
    hp = lax.Precision.HIGHEST
    lb_re = lr[:qs, :, :, None] * bb_re - li[:qs, :, :, None] * bb_im
    lb_im = lr[:qs, :, :, None] * bb_im + li[:qs, :, :, None] * bb_re
    kern = (jnp.einsum('ghp,kgpj->kghj', c_re, lb_re, precision=hp)
            - jnp.einsum('ghp,kgpj->kghj', c_im, lb_im, precision=hp))
    tt = jnp.arange(qs)
    lag = tt[:, None] - tt[None, :]
    m = jnp.where((lag >= 0)[:, :, None, None, None],
                  kern[jnp.clip(lag, 0, qs - 1)], 0.0)
    m = jnp.transpose(m, (2, 0, 3, 1, 4)).reshape(g, r, r)
    rev_re = lr[:qs][::-1]
    rev_im = li[:qs][::-1]
    bp_re = rev_re[..., None] * bb_re - rev_im[..., None] * bb_im
    bp_im = rev_re[..., None] * bb_im + rev_im[..., None] * bb_re
    bp_re = jnp.transpose(bp_re, (1, 2, 0, 3)).reshape(g, p, r)
    bp_im = jnp.transpose(bp_im, (1, 2, 0, 3)).reshape(g, p, r)
    fr = lr[1:, :, None, :]
    fi = li[1:, :, None, :]
    cy_re = c_re[None] * fr - c_im[None] * fi
    cy_im = -(c_re[None] * fi + c_im[None] * fr)
    cy_re = jnp.transpose(cy_re, (1, 0, 2, 3)).reshape(g, r, p)
    cy_im = jnp.transpose(cy_im, (1, 0, 2, 3)).reshape(g, r, p)

    np_ = g // 2
    z_pr = jnp.zeros((np_, p, r), F32)
    e = lambda a: a.reshape(np_, 2, *a.shape[1:])
    bre, bim = e(bp_re), e(bp_im)
    bp = jnp.concatenate([
        jnp.concatenate([bre[:, 0], z_pr], axis=2), jnp.concatenate([z_pr, bre[:, 1]], axis=2),
        jnp.concatenate([bim[:, 0], z_pr], axis=2), jnp.concatenate([z_pr, bim[:, 1]], axis=2)], axis=1)
    z_rp = jnp.zeros((np_, r, p), F32)
    cre, cim = e(cy_re), e(cy_im)
    cp = jnp.concatenate([
        jnp.concatenate([cre[:, 0], z_rp, cim[:, 0], z_rp], axis=2),
        jnp.concatenate([z_rp, cre[:, 1], z_rp, cim[:, 1]], axis=2)], axis=1)
    lam_q = jnp.stack([lr[qs].reshape(np_, 2 * p), li[qs].reshape(np_, 2 * p)], axis=1)
    dcol = jnp.broadcast_to(d.reshape(np_, 2, 1, h), (np_, 2, qs, h)).reshape(np_, 2 * r, 1)
    bp_t = jnp.transpose(bp, (0, 2, 1))
    return e(m).astype(BF16), bp_t.astype(BF16), cp.astype(BF16), lam_q, dcol


def _gelu_tanh(x):
    return 0.5 * x * (1.0 + jnp.tanh(math.sqrt(2.0 / math.pi) * (x + 0.044715 * (x * x * x))))


def _s5_kernel(u_ref, m_ref, bp_ref, cp_ref, lam_ref, d_ref, s0_ref, y_ref, sout_ref,
               loc_re, loc_im, st_re, st_im, *, nb, nc):
    r = m_ref.shape[2]
    pw = lam_ref.shape[2]
    u = u_ref[0]
    loc_re[...] = _dot_tn(u, bp_ref[0, :, :pw])
    loc_im[...] = _dot_tn(u, bp_ref[0, :, pw:])
    ar = lam_ref[0, 0:1, :]
    ai = lam_ref[0, 1:2, :]
    sr = s0_ref[0, 0]
    si = s0_ref[0, 1]
    for c in range(nc):
        rows = pl.ds(c, nb, stride=nc) if nc > 1 else pl.ds(0, nb)
        st_re[rows, :] = sr
        st_im[rows, :] = si
        sr, si = (ar * sr - ai * si + loc_re[rows, :], ar * si + ai * sr + loc_im[rows, :])
    sout_ref[0, 0] = sr
    sout_ref[0, 1] = si
    y_off = (_dot_nt(cp_ref[0, :, :pw], st_re[...].astype(BF16))
             + _dot_nt(cp_ref[0, :, pw:], st_im[...].astype(BF16)))
    y_loc = jnp.concatenate([_dot(m_ref[0, 0], u[:r]), _dot(m_ref[0, 1], u[r:])], axis=0)
    y = y_loc + y_off + d_ref[0] * u.astype(F32)
    y_ref[0] = _gelu_tanh(y).astype(y_ref.dtype)


def _s5(u, s0_re, s0_im, tables, qs):
    m, bp, cp, lam_q, dcol = tables
    b, l, _ = u.shape
    nc = l // qs
    np_ = S5_GROUPS // 2
    r = qs * S5_GROUP_CH
    nbc = b * nc
    pw = 2 * S5_STATE
    ut = jnp.transpose(u.reshape(b, nc, qs, np_, 2, S5_GROUP_CH), (3, 4, 2, 5, 0, 1)).reshape(np_, 2 * r, nbc)
    pack = lambda s: jnp.transpose(s.reshape(b, np_, pw), (1, 0, 2))
    s0 = jnp.stack([pack(s0_re), pack(s0_im)], axis=1)
    kern = functools.partial(_s5_kernel, nb=b, nc=nc)
    blk = lambda *shape: pl.BlockSpec((1,) + shape, lambda i: (i,) + (0,) * len(shape))
    yt, sout = pl.pallas_call(
        kern,
        grid=(np_,),
        in_specs=[blk(2 * r, nbc), blk(2, r, r), blk(2 * r, 2 * pw), blk(2 * r, 2 * pw),
                  blk(2, pw), blk(2 * r, 1), blk(2, b, pw)],
        out_specs=[blk(2 * r, nbc), blk(2, b, pw)],
        out_shape=[jax.ShapeDtypeStruct((np_, 2 * r, nbc), BF16),
                   jax.ShapeDtypeStruct((np_, 2, b, pw), F32)],
        scratch_shapes=[pltpu.VMEM((nbc, pw), F32)] * 4,
        compiler_params=_cparams("parallel"),
    )(ut, m, bp, cp, lam_q, dcol, s0)
    y = jnp.transpose(yt.reshape(np_, 2, qs, S5_GROUP_CH, b, nc), (4, 5, 2, 0, 1, 3)).reshape(b, l, D_S5)
    unpack = lambda s: jnp.transpose(s, (1, 0, 2)).reshape(b, S5_GROUPS, S5_STATE)
    return y, unpack(sout[:, 0]), unpack(sout[:, 1])


def _mix_kernel(ym_ref, yg_ref, x_ref, wglu_ref, bglu_ref, wmo_ref, g_ref, b_ref, o_ref):
    yg = yg_ref[...]
    gate = _sigmoid(_dot(yg, wglu_ref[...]) + bglu_ref[...])
    ys = (yg.astype(F32) * gate).astype(BF16)
    mix = _dot(ym_ref[...], wmo_ref[0:D_SSM, :]) + _dot(ys, wmo_ref[D_SSM:, :])
    o_ref[...] = _layer_norm(ALPHA * x_ref[...] + mix, g_ref[...], b_ref[...])


def _mix(ym, yg, x2d, wglu, bglu, wmo, g, b, tm):
    t = x2d.shape[0]
    full = lambda i: (0, 0)
    row = lambda i: (i, 0)
    return pl.pallas_call(
        _mix_kernel,
        grid=(t // tm,),
        in_specs=[pl.BlockSpec((tm, D_SSM), row), pl.BlockSpec((tm, D_S5), row),
                  pl.BlockSpec((tm, D_MODEL), row),
                  pl.BlockSpec(wglu.shape, full), pl.BlockSpec(bglu.shape, full),
                  pl.BlockSpec(wmo.shape, full), pl.BlockSpec(g.shape, full), pl.BlockSpec(b.shape, full)],
        out_specs=pl.BlockSpec((tm, D_MODEL), row),
        out_shape=jax.ShapeDtypeStruct((t, D_MODEL), F32),
        compiler_params=_cparams("parallel"),
    )(ym, yg, x2d, wglu, bglu, wmo, g, b)


def _memkv_kernel(m_ref, wk_ref, wv_ref, k_ref, v_ref):
    m = m_ref[...].astype(BF16)
    k_ref[...] = _dot(m, wk_ref[...])
    v_ref[...] = _dot(m, wv_ref[...])


def _memkv(mem2d, wk, wv, tm):
    t = mem2d.shape[0]
    full = lambda i: (0, 0)
    row = lambda i: (i, 0)
    return pl.pallas_call(
        _memkv_kernel,
        grid=(t // tm,),
        in_specs=[pl.BlockSpec((tm, D_MODEL), row), pl.BlockSpec(wk.shape, full), pl.BlockSpec(wv.shape, full)],
        out_specs=[pl.BlockSpec((tm, D_MODEL), row), pl.BlockSpec((tm, D_MODEL), row)],
        out_shape=[jax.ShapeDtypeStruct((t, D_MODEL), F32), jax.ShapeDtypeStruct((t, D_MODEL), F32)],
        compiler_params=_cparams("parallel"),
    )(mem2d, wk, wv)


def _attn_kernel(x_ref, k_ref, v_ref, wq_ref, wo_ref, g_ref, b_ref, o_ref):
    x = x_ref[0]
    q = _dot(x.astype(BF16), wq_ref[...]).astype(BF16)
    k = k_ref[0].astype(BF16)
    v = v_ref[0].astype(BF16)
    scale = MEM_HEAD_DIM ** -0.5
    outs = []
    for h in range(MEM_HEADS):
        sl = slice(h * MEM_HEAD_DIM, (h + 1) * MEM_HEAD_DIM)
        s = _dot_nt(q[:, sl], k[:, sl]) * scale
        s = s - jnp.max(s, axis=-1, keepdims=True)
        p = jnp.exp(s)
        p = p / jnp.sum(p, axis=-1, keepdims=True)
        outs.append(_dot(p.astype(BF16), v[:, sl]))
    o = jnp.concatenate(outs, axis=1).astype(BF16)
    att = _dot(o, wo_ref[...])
    o_ref[0] = _layer_norm(ALPHA * x + att, g_ref[...], b_ref[...])


def _attn(x3d, k, v, wq, wo, g, b, tq):
    bsz, l, _ = x3d.shape
    m = k.shape[1]
    full = lambda i, j: (0, 0)
    seq = lambda i, j: (i, j, 0)
    per_b = lambda i, j: (i, 0, 0)
    return pl.pallas_call(
        _attn_kernel,
        grid=(bsz, l // tq),
        in_specs=[pl.BlockSpec((1, tq, D_MODEL), seq),
                  pl.BlockSpec((1, m, D_MODEL), per_b), pl.BlockSpec((1, m, D_MODEL), per_b),
                  pl.BlockSpec(wq.shape, full), pl.BlockSpec(wo.shape, full),
                  pl.BlockSpec(g.shape, full), pl.BlockSpec(b.shape, full)],
        out_specs=pl.BlockSpec((1, tq, D_MODEL), seq),
        out_shape=jax.ShapeDtypeStruct((bsz, l, D_MODEL), F32),
        compiler_params=_cparams("parallel", "parallel"),
    )(x3d, k, v, wq, wo, g, b)


def _router_kernel(x_ref, wh_ref, wl_ref, br_ref, comb_ref, pos_ref, cnt_ref):
    x = x_ref[...]
    tm = x.shape[0]
    xh, xl = _split(x)
    wh = wh_ref[...]
    logits = _dot_nt(wh, xh) + _dot_nt(wh, xl) + _dot_nt(wl_ref[...], xh)
    s = _sigmoid(logits)
    sb = s + br_ref[...]
    per = N_EXPERTS // N_EXPERT_GROUPS
    neg = -jnp.inf

    sb3 = sb.reshape(N_EXPERT_GROUPS, per, tm)
    mem = lax.broadcasted_iota(jnp.int32, sb3.shape, 1)
    m1 = jnp.max(sb3, axis=1, keepdims=True)
    first = jnp.min(jnp.where(sb3 == m1, mem, per), axis=1, keepdims=True)
    m2 = jnp.max(jnp.where(mem == first, neg, sb3), axis=1, keepdims=True)
    gs = (m1 + m2).reshape(N_EXPERT_GROUPS, tm)

    def rank_rows(v):
        n = v.shape[0]
        idx = lax.broadcasted_iota(jnp.int32, v.shape, 0)
        rank = jnp.zeros(v.shape, jnp.int32)
        for j in range(n):
            rj = v[j:j + 1, :]
            beats = (rj > v) | ((rj == v) & (idx > j))
            rank = rank + beats.astype(jnp.int32)
        return rank

    gsel = rank_rows(gs) < TOPK_GROUPS
    emask = jnp.broadcast_to(gsel.reshape(N_EXPERT_GROUPS, 1, tm), sb3.shape).reshape(N_EXPERTS, tm)
    sm = jnp.where(emask, sb, neg)
    sel = (rank_rows(sm) < TOP_K) & emask
    gates = jnp.where(sel, s, 0.0)
    denom = jnp.sum(gates, axis=0, keepdims=True)
    comb_ref[...] = gates / denom * ROUTE_SCALE

    self = jnp.where(sel, 1.0, 0.0)
    before = (lax.broadcasted_iota(jnp.int32, (tm, tm), 0)
              < lax.broadcasted_iota(jnp.int32, (tm, tm), 1)).astype(BF16)
    rank_tok = _dot(self.astype(BF16), before)
    pos_ref[...] = jnp.where(sel, rank_tok.astype(jnp.int32), -1)
    cnt = jnp.sum(self, axis=1, keepdims=True).astype(jnp.int32)
    cnt_ref[0] = jnp.broadcast_to(cnt, cnt_ref.shape[1:])


def _router(x2d, wr_hi_t, wr_lo_t, br_col, tm):
    t = x2d.shape[0]
    nt = t // tm
    full = lambda i: (0, 0)
    return pl.pallas_call(
        _router_kernel,
        grid=(nt,),
        in_specs=[pl.BlockSpec((tm, D_MODEL), lambda i: (i, 0)), pl.BlockSpec(wr_hi_t.shape, full),
                  pl.BlockSpec(wr_lo_t.shape, full), pl.BlockSpec(br_col.shape, full)],
        out_specs=[pl.BlockSpec((N_EXPERTS, tm), lambda i: (0, i)),
                   pl.BlockSpec((N_EXPERTS, tm), lambda i: (0, i)),
                   pl.BlockSpec((1, N_EXPERTS, 128), lambda i: (i, 0, 0))],
        out_shape=[jax.ShapeDtypeStruct((N_EXPERTS, t), F32),
                   jax.ShapeDtypeStruct((N_EXPERTS, t), jnp.int32),
                   jax.ShapeDtypeStruct((nt, N_EXPERTS, 128), jnp.int32)],
        compiler_params=_cparams("parallel"),
    )(x2d, wr_hi_t, wr_lo_t, br_col)


def _moe_rows(t, nt):
    worst = TOP_K * t + nt * N_EXPERTS * (MOE_SEG - 1) + N_EXPERTS * (MOE_TR - 1)
    return -(-worst // MOE_TR) * MOE_TR


def _moe_local_rows(tm):
    worst = TOP_K * tm + N_EXPERTS * (MOE_SEG - 1)
    return -(-worst // 512) * 512


def _moe_plan(cnt, tm):
    nt = cnt.shape[0]
    npad = (cnt + (MOE_SEG - 1)) // MOE_SEG * MOE_SEG
    loc_off = jnp.cumsum(npad, axis=1) - npad
    tot = jnp.sum(npad, axis=0)
    totp = (tot + (MOE_TR - 1)) // MOE_TR * MOE_TR
    gend = jnp.cumsum(totp)
    goff = gend - totp
    seg_off = goff[None, :] + jnp.cumsum(npad, axis=0) - npad
    n_row_tiles = _moe_rows(nt * tm, nt) // MOE_TR
    tile_start = jnp.arange(n_row_tiles, dtype=jnp.int32) * MOE_TR
    ended = (gend[None, :] <= tile_start[:, None]).astype(jnp.int32)
    tile_expert = jnp.minimum(jnp.sum(ended, axis=1), N_EXPERTS - 1)
    n_valid = (gend[-1] // MOE_TR).reshape(1)
    i32 = lambda a: a.astype(jnp.int32)
    return dict(npad=i32(npad.reshape(-1)), loc_off=i32(loc_off.reshape(-1)), seg_off=i32(seg_off.reshape(-1)),
                fill_off=i32(goff + tot), fill_n=i32((totp - tot) // MOE_SEG),
                tile_expert=i32(tile_expert), n_valid=i32(n_valid))


def _for_each_segment(i, npad_ref, loc_ref, seg_ref, body):
    def per_expert(e, n):
        k = i * N_EXPERTS + e
        nch = npad_ref[k] // MOE_SEG
        base = loc_ref[k]
        dst = seg_ref[k]

        def chunk(j, c):
            body(e, j, pl.multiple_of(base + j * MOE_SEG, MOE_SEG), pl.multiple_of(dst + j * MOE_SEG, MOE_SEG))
            return c

        lax.fori_loop(0, nch, chunk, 0)
        return n + nch

    return lax.fori_loop(0, N_EXPERTS, per_expert, 0)


def _one_hot_rows(pos_ref, e, j, tm, value):
    row = pos_ref[pl.ds(e, 1), :]
    want = lax.broadcasted_iota(jnp.int32, (MOE_SEG, tm), 0) + j * MOE_SEG
    return jnp.where(row == want, value, 0.0).astype(BF16)


def _moe_gather_kernel(npad_ref, loc_ref, seg_ref, foff_ref, fn_ref, x_ref, pos_ref, xs_hbm,
                       p_scr, xs_scr, zero_scr, sem):
    i = pl.program_id(0)
    tm = x_ref.shape[0]
    p_scr[...] = jnp.zeros_like(p_scr)

    def put(e, j, lrow, grow):
        p_scr[pl.ds(lrow, MOE_SEG), :] = _one_hot_rows(pos_ref, e, j, tm, 1.0)

    _for_each_segment(i, npad_ref, loc_ref, seg_ref, put)
    xs_scr[...] = _dot(p_scr[...], x_ref[...].astype(BF16)).astype(BF16)

    def send(e, j, lrow, grow):
        pltpu.make_async_copy(xs_scr.at[pl.ds(lrow, MOE_SEG)], xs_hbm.at[pl.ds(grow, MOE_SEG)], sem).start()

    n_sent = _for_each_segment(i, npad_ref, loc_ref, seg_ref, send)

    def wait_one(k, c):
        pltpu.make_async_copy(xs_scr.at[pl.ds(0, MOE_SEG)], xs_hbm.at[pl.ds(0, MOE_SEG)], sem).wait()
        return c

    lax.fori_loop(0, n_sent, wait_one, 0)

    @pl.when(i == pl.num_programs(0) - 1)
    def _():
        zero_scr[...] = jnp.zeros_like(zero_scr)

        def per_expert(e, n):
            def chunk(j, c):
                grow = pl.multiple_of(foff_ref[e] + j * MOE_SEG, MOE_SEG)
                pltpu.make_async_copy(zero_scr, xs_hbm.at[pl.ds(grow, MOE_SEG)], sem).start()
                return c
            lax.fori_loop(0, fn_ref[e], chunk, 0)
            return n + fn_ref[e]

        n_fill = lax.fori_loop(0, N_EXPERTS, per_expert, 0)
        lax.fori_loop(0, n_fill, wait_one, 0)


def _moe_gather(plan, x2d, pos, tm):
    t = x2d.shape[0]
    nt = t // tm
    rows = _moe_rows(t, nt)
    rl = _moe_local_rows(tm)
    return pl.pallas_call(
        _moe_gather_kernel,
        grid_spec=pltpu.PrefetchScalarGridSpec(
            num_scalar_prefetch=5,
            grid=(nt,),
            in_specs=[pl.BlockSpec((tm, D_MODEL), lambda i, *_: (i, 0)),
                      pl.BlockSpec((N_EXPERTS, tm), lambda i, *_: (0, i))],
            out_specs=pl.BlockSpec(memory_space=pl.ANY),
            scratch_shapes=[pltpu.VMEM((rl, tm), BF16), pltpu.VMEM((rl, D_MODEL), BF16),
                            pltpu.VMEM((MOE_SEG, D_MODEL), BF16), pltpu.SemaphoreType.DMA]),
        out_shape=jax.ShapeDtypeStruct((rows, D_MODEL), BF16),
        compiler_params=_cparams("arbitrary"),
    )(plan['npad'], plan['loc_off'], plan['seg_off'], plan['fill_off'], plan['fill_n'], x2d, pos)


def _moe_expert_kernel(te_ref, nv_ref, xs_ref, wg_ref, wu_ref, wd_ref, ys_ref, wg_s, wu_s, wd_s):
    j = pl.program_id(0)
    changed = te_ref[j] != te_ref[jnp.maximum(j - 1, 0)]

    @pl.when((j == 0) | changed)
    def _():
        wg_s[...] = wg_ref[0].astype(BF16)
        wu_s[...] = wu_ref[0].astype(BF16)
        wd_s[...] = wd_ref[0].astype(BF16)

    @pl.when(j < nv_ref[0])
    def _():
        x = xs_ref[...]
        hid = _silu(_dot(x, wg_s[...])) * _dot(x, wu_s[...])
        ys_ref[...] = _dot(hid.astype(BF16), wd_s[...]).astype(ys_ref.dtype)


def _moe_experts(plan, xs, wg, wu, wd):
    rows = xs.shape[0]
    de = wg.shape[2]
    row_tile = lambda j, te, nv: (jnp.minimum(j, nv[0] - 1), 0)
    expert = lambda j, te, nv: (te[jnp.minimum(j, nv[0] - 1)], 0, 0)
    return pl.pallas_call(
        _moe_expert_kernel,
        grid_spec=pltpu.PrefetchScalarGridSpec(
            num_scalar_prefetch=2,
            grid=(rows // MOE_TR,),
            in_specs=[pl.BlockSpec((MOE_TR, D_MODEL), row_tile),
                      pl.BlockSpec((1, D_MODEL, de), expert), pl.BlockSpec((1, D_MODEL, de), expert),
                      pl.BlockSpec((1, de, D_MODEL), expert)],
            out_specs=pl.BlockSpec((MOE_TR, D_MODEL), row_tile),
            scratch_shapes=[pltpu.VMEM((D_MODEL, de), BF16), pltpu.VMEM((D_MODEL, de), BF16),
                            pltpu.VMEM((de, D_MODEL), BF16)]),
        out_shape=jax.ShapeDtypeStruct((rows, D_MODEL), BF16),
        compiler_params=_cparams("arbitrary"),
    )(plan['tile_expert'], plan['n_valid'], xs, wg, wu, wd)


def _moe_combine_kernel(npad_ref, loc_ref, seg_ref, x_ref, pos_ref, comb_ref, ys_hbm, sg_ref, su_ref,
                        sd_ref, g_ref, b_ref, o_ref, pg_scr, ys_scr, sem):
    i = pl.program_id(0)
    tm = x_ref.shape[0]

    @pl.when(i == 0)
    def _():
        ys_scr[...] = jnp.zeros_like(ys_scr)

    def fetch(e, j, lrow, grow):
        pltpu.make_async_copy(ys_hbm.at[pl.ds(grow, MOE_SEG)], ys_scr.at[pl.ds(lrow, MOE_SEG)], sem).start()

    n_fetch = _for_each_segment(i, npad_ref, loc_ref, seg_ref, fetch)

    pg_scr[...] = jnp.zeros_like(pg_scr)

    def put(e, j, lrow, grow):
        pg_scr[pl.ds(lrow, MOE_SEG), :] = _one_hot_rows(pos_ref, e, j, tm, comb_ref[pl.ds(e, 1), :])

    _for_each_segment(i, npad_ref, loc_ref, seg_ref, put)

    x = x_ref[...]
    xb = x.astype(BF16)
    hs = _silu(_dot(xb, sg_ref[...])) * _dot(xb, su_ref[...])
    shared = _dot(hs.astype(BF16), sd_ref[...])

    def wait_one(k, c):
        pltpu.make_async_copy(ys_hbm.at[pl.ds(0, MOE_SEG)], ys_scr.at[pl.ds(0, MOE_SEG)], sem).wait()
        return c

    lax.fori_loop(0, n_fetch, wait_one, 0)
    routed = _dot_tn(pg_scr[...], ys_scr[...])
    o_ref[...] = _layer_norm(ALPHA * x + (routed + shared), g_ref[...], b_ref[...])


def _moe_combine(plan, x2d, pos, comb, ys, sg, su, sd, g, b, tm):
    t = x2d.shape[0]
    nt = t // tm
    rl = _moe_local_rows(tm)
    full = lambda i, *_: (0, 0)
    return pl.pallas_call(
        _moe_combine_kernel,
        grid_spec=pltpu.PrefetchScalarGridSpec(
            num_scalar_prefetch=3,
            grid=(nt,),
            in_specs=[pl.BlockSpec((tm, D_MODEL), lambda i, *_: (i, 0)),
                      pl.BlockSpec((N_EXPERTS, tm), lambda i, *_: (0, i)),
                      pl.BlockSpec((N_EXPERTS, tm), lambda i, *_: (0, i)),
                      pl.BlockSpec(memory_space=pl.ANY),
                      pl.BlockSpec(sg.shape, full), pl.BlockSpec(su.shape, full), pl.BlockSpec(sd.shape, full),
                      pl.BlockSpec(g.shape, full), pl.BlockSpec(b.shape, full)],
            out_specs=pl.BlockSpec((tm, D_MODEL), lambda i, *_: (i, 0)),
            scratch_shapes=[pltpu.VMEM((rl, tm), BF16), pltpu.VMEM((rl, D_MODEL), BF16),
                            pltpu.SemaphoreType.DMA]),
        out_shape=jax.ShapeDtypeStruct((t, D_MODEL), F32),
        compiler_params=_cparams("arbitrary"),
    )(plan['npad'], plan['loc_off'], plan['seg_off'], x2d, pos, comb, ys, sg, su, sd, g, b)


def _moe(x2d, w, tm):
    comb, pos, cnt = _router(x2d, w['wr_hi_t'], w['wr_lo_t'], w['br_col'], tm)
    plan = _moe_plan(cnt[:, :, 0], tm)
    xs = _moe_gather(plan, x2d, pos, tm)
    ys = _moe_experts(plan, xs, w['w_e_gate'], w['w_e_up'], w['w_e_down'])
    return _moe_combine(plan, x2d, pos, comb, ys, w['w_s_gate'], w['w_s_up'], w['w_s_down'],
                        w['ln3_g'], w['ln3_b'], tm)


def _layer(x3d, mem_k, mem_v, conv_st, ssm_st, s5_re, s5_im, w, *, l_valid, s5_chunk, tm, tq):
    bsz, lp, _ = x3d.shape
    t = bsz * lp
    x2d = x3d.reshape(t, D_MODEL)
    tm = min(tm, t)
    tq = min(tq, lp)
    z, xbc, dt, u = _in_proj(x2d, w['wz'], w['wx'], w['wd'], w['wu'], tm)
    q_in = min(lp, SSD_CHUNK)
    h0_t = jnp.transpose(ssm_st.reshape(bsz, D_SSM, SSM_STATE), (0, 2, 1))
    ym, conv_new, h_t = _ssd(xbc.reshape(bsz, lp, D_CONV), z.reshape(bsz, lp, D_SSM),
                             dt.reshape(bsz, lp, SSM_HEADS), conv_st, h0_t,
                             w['conv_w'], w['conv_b'], w['dt_bias'], w['a_log'], w['d_skip_x'],
                             w['norm_ssm'], q_in, min(l_valid, q_in))
    ssm_new = jnp.transpose(h_t, (0, 2, 1)).reshape(bsz, SSM_HEADS, SSM_HEAD_DIM, SSM_STATE)
    u3 = u.reshape(bsz, lp, D_S5)[:, :l_valid]
    yg, s5_re_new, s5_im_new = _s5(u3, s5_re, s5_im, w['s5_tables'][s5_chunk], s5_chunk)
    if lp > l_valid:
        yg = jnp.pad(yg, ((0, 0), (0, lp - l_valid), (0, 0)))
    x1 = _mix(ym.reshape(t, D_SSM), yg.reshape(t, D_S5), x2d, w['w_glu'], w['b_glu'], w['w_mix_out'],
              w['ln1_g'], w['ln1_b'], tm)
    x2 = _attn(x1.reshape(bsz, lp, D_MODEL), mem_k, mem_v, w['w_q'], w['w_o'], w['ln2_g'], w['ln2_b'], tq)
    return x2.reshape(t, D_MODEL), conv_new, ssm_new, s5_re_new, s5_im_new


def kernel(x_prompt, x_sample, mem_prompt, state_conv, state_ssm, state_s5_re, state_s5_im, cache_mem_k, cache_mem_v, w_in, conv_w, conv_b, dt_bias, a_log, d_ssm, norm_ssm, s5_a_re, s5_a_im, s5_log_step, s5_b_re, s5_b_im, s5_c_re, s5_c_im, s5_d, w_glu, b_glu, w_mix_out, ln1_g, ln1_b, w_q, w_k, w_v, w_o, ln2_g, ln2_b, w_router, b_router, w_e_gate, w_e_up, w_e_down, w_s_gate, w_s_up, w_s_down, ln3_g, ln3_b):
    assert w_in.shape[0] == DEPTH
    bp, lp, _ = x_prompt.shape
    bs, ls, _ = x_sample.shape
    row = lambda a: a.reshape(1, -1)
    bf = lambda a: a.astype(BF16)
    w_in0 = w_in[0]
    o1 = D_SSM
    o2 = o1 + D_CONV
    o3 = o2 + SSM_HEADS
    wr_hi, wr_lo = _split(w_router[0].T)
    s5_args = (s5_a_re[0], s5_a_im[0], s5_log_step[0], s5_b_re[0], s5_b_im[0], s5_c_re[0], s5_c_im[0], s5_d[0])
    s5_chunk_p = math.gcd(lp, 16)
    w = {
        'wz': bf(w_in0[:, :o1]), 'wx': bf(w_in0[:, o1:o2]), 'wd': bf(w_in0[:, o2:o3]), 'wu': bf(w_in0[:, o3:]),
        'conv_w': conv_w[0], 'conv_b': row(conv_b[0]), 'dt_bias': row(dt_bias[0]), 'a_log': row(a_log[0]),
        'd_skip_x': row(jnp.repeat(d_ssm[0], SSM_HEAD_DIM)), 'norm_ssm': row(norm_ssm[0]),
        's5_tables': {q: _s5_tables(*s5_args, q) for q in sorted({s5_chunk_p, ls})},
        'w_glu': bf(w_glu[0]), 'b_glu': row(b_glu[0]), 'w_mix_out': bf(w_mix_out[0]),
        'ln1_g': row(ln1_g[0]), 'ln1_b': row(ln1_b[0]),
        'w_q': bf(w_q[0]), 'w_o': bf(w_o[0]), 'ln2_g': row(ln2_g[0]), 'ln2_b': row(ln2_b[0]),
        'wr_hi_t': wr_hi, 'wr_lo_t': wr_lo, 'br_col': b_router[0].reshape(-1, 1),
        'w_e_gate': w_e_gate[0], 'w_e_up': w_e_up[0], 'w_e_down': w_e_down[0],
        'w_s_gate': bf(w_s_gate[0]), 'w_s_up': bf(w_s_up[0]), 'w_s_down': bf(w_s_down[0]),
        'ln3_g': row(ln3_g[0]), 'ln3_b': row(ln3_b[0]),
    }

    m_tok = mem_prompt.shape[1]
    mk2d, mv2d = _memkv(mem_prompt.reshape(bp * m_tok, D_MODEL), bf(w_k[0]), bf(w_v[0]), 512)
    mk = mk2d.reshape(bp, m_tok, D_MODEL)
    mv = mv2d.reshape(bp, m_tok, D_MODEL)
    zeros = lambda *s: jnp.zeros(s, F32)
    x2p, conv_p, ssm_p, s5r_p, s5i_p = _layer(
        x_prompt, mk, mv, zeros(bp, CONV_W - 1, D_CONV), zeros(bp, SSM_HEADS, SSM_HEAD_DIM, SSM_STATE),
        zeros(bp, S5_GROUPS, S5_STATE), zeros(bp, S5_GROUPS, S5_STATE), w,
        l_valid=lp, s5_chunk=s5_chunk_p, tm=512, tq=512)

    xs_pad = jnp.pad(x_sample, ((0, 0), (0, SAMPLE_PAD - ls), (0, 0)))
    ck = cache_mem_k[0].reshape(bs, m_tok, D_MODEL)
    cv = cache_mem_v[0].reshape(bs, m_tok, D_MODEL)
    x2s, conv_s, ssm_s, s5r_s, s5i_s = _layer(
        xs_pad, ck, cv, state_conv[0], state_ssm[0], state_s5_re[0], state_s5_im[0], w,
        l_valid=ls, s5_chunk=ls, tm=512, tq=SAMPLE_PAD)

    tp = bp * lp
    y_all = _moe(jnp.concatenate([x2p, x2s], axis=0), w, MOE_TM)
    yp = y_all[:tp].reshape(bp, lp, D_MODEL)
    ys = y_all[tp:].reshape(bs, SAMPLE_PAD, D_MODEL)[:, :ls]

    kv_shape = (1, bp, m_tok, MEM_HEADS, MEM_HEAD_DIM)
    return (yp, ys, conv_p[None], ssm_p[None], s5r_p[None], s5i_p[None],
            mk.reshape(kv_shape), mv.reshape(kv_shape),
            conv_s[None], ssm_s[None], s5r_s[None], s5i_s[None])
```

```python
import functools
import math

import jax
import jax.numpy as jnp
from jax import lax
from jax.experimental import pallas as pl
from jax.experimental.pallas import tpu as pltpu

F32 = jnp.float32
BF16 = jnp.bfloat16

D_MODEL = 1024
D_SSM = 1024
SSM_HEAD_DIM = 64
SSM_HEADS = D_SSM // SSM_HEAD_DIM
SSM_GROUPS = 2
SSM_STATE = 128
CONV_W = 4
D_CONV = D_SSM + 2 * SSM_GROUPS * SSM_STATE
D_S5 = 1024
S5_GROUP_CH = 16
S5_GROUPS = D_S5 // S5_GROUP_CH
S5_STATE = 64
MEM_HEADS = 4
MEM_HEAD_DIM = D_MODEL // MEM_HEADS
N_EXPERTS = 64
N_EXPERT_GROUPS = 8
TOPK_GROUPS = 4
TOP_K = 6
ROUTE_SCALE = 2.5
DEPTH = 1
ALPHA = (2.0 * DEPTH) ** 0.25
LN_EPS = 1e-5
RMS_EPS = 1e-5

SSD_CHUNK = 128
SAMPLE_PAD = 8
SLOT_BLK = 8
SLOT_TN = 128
MOE_TM = 512
MOE_TR = 512
MOE_SEG = 16
VMEM_LIMIT = 48 * 1024 * 1024


def _cparams(*sem):
    return pltpu.CompilerParams(dimension_semantics=sem, vmem_limit_bytes=VMEM_LIMIT)


def _dot(a, b):
    return jnp.dot(a, b, preferred_element_type=F32)


def _dot_nt(a, b):
    return lax.dot_general(a, b, (((1,), (1,)), ((), ())), preferred_element_type=F32)


def _dot_tn(a, b):
    return lax.dot_general(a, b, (((0,), (0,)), ((), ())), preferred_element_type=F32)


def _split(v):
    hi = v.astype(BF16)
    lo = (v - hi.astype(F32)).astype(BF16)
    return hi, lo


def _silu(x):
    return x * (1.0 / (1.0 + jnp.exp(-x)))


def _sigmoid(x):
    return 1.0 / (1.0 + jnp.exp(-x))


def _softplus(x):
    return jnp.maximum(x, 0.0) + jnp.log(1.0 + jnp.exp(-jnp.abs(x)))


def _layer_norm(x, g, b):
    mu = jnp.mean(x, axis=-1, keepdims=True)
    xc = x - mu
    var = jnp.mean(xc * xc, axis=-1, keepdims=True)
    return xc * lax.rsqrt(var + LN_EPS) * g + b


def _in_proj_kernel(x_ref, wz_ref, wx_ref, wd_ref, z_ref, xbc_ref, dt_ref):
    x = x_ref[...].astype(BF16)
    z_ref[...] = _dot(x, wz_ref[...]).astype(z_ref.dtype)
    xbc_ref[...] = _dot(x, wx_ref[...])
    dt_ref[...] = _dot(x, wd_ref[...])


def _in_proj(x2d, wz, wx, wd, tm):
    t = x2d.shape[0]
    full = lambda i: (0, 0)
    row = lambda i: (i, 0)
    return pl.pallas_call(
        _in_proj_kernel,
        grid=(t // tm,),
        in_specs=[pl.BlockSpec((tm, D_MODEL), row),
                  pl.BlockSpec(wz.shape, full), pl.BlockSpec(wx.shape, full), pl.BlockSpec(wd.shape, full)],
        out_specs=[pl.BlockSpec((tm, D_SSM), row), pl.BlockSpec((tm, D_CONV), row),
                   pl.BlockSpec((tm, SSM_HEADS), row)],
        out_shape=[jax.ShapeDtypeStruct((t, D_SSM), BF16), jax.ShapeDtypeStruct((t, D_CONV), F32),
                   jax.ShapeDtypeStruct((t, SSM_HEADS), F32)],
        compiler_params=_cparams("parallel"),
    )(x2d, wz, wx, wd)


def _in_proj_u_kernel(x_ref, wu_ref, u_ref):
    np_, _, qb, h, tn = u_ref.shape
    for t in range(qb):
        ut = _dot_nt(wu_ref[...], x_ref[:, t, :].astype(BF16))
        u_ref[:, :, t] = ut.astype(u_ref.dtype).reshape(np_, 2, h, tn)


def _in_proj_u(x3, wu_t, qs, tn):
    nbc = x3.shape[0]
    np_ = S5_GROUPS // 2
    qb = min(qs, SLOT_BLK)
    return pl.pallas_call(
        _in_proj_u_kernel,
        grid=(qs // qb, nbc // tn),
        in_specs=[pl.BlockSpec((tn, SLOT_BLK, D_MODEL), lambda t, j: (j, t, 0)),
                  pl.BlockSpec(wu_t.shape, lambda t, j: (0, 0))],
        out_specs=pl.BlockSpec((np_, 2, qb, S5_GROUP_CH, tn), lambda t, j: (0, 0, t, 0, j)),
        out_shape=jax.ShapeDtypeStruct((np_, 2, qs, S5_GROUP_CH, nbc), BF16),
        compiler_params=_cparams("parallel", "parallel"),
    )(x3, wu_t)


def _ssd_kernel(xbc_ref, z_ref, dt_ref, cst_ref, h0_ref, cw_ref, cb_ref, dtb_ref, alog_ref,
                dskip_ref, nw_ref, y_ref, cnew_ref, hout_ref, h_scr, xp_scr, dt_scr,
                *, q, q_in, l_valid):
    c = pl.program_id(1)
    halo = CONV_W - 1
    base = 8

    @pl.when(c == 0)
    def _():
        h_scr[...] = h0_ref[0]
        xp_scr[...] = jnp.zeros_like(xp_scr)
        dt_scr[...] = jnp.zeros_like(dt_scr)
        xp_scr[base - halo:base, :] = cst_ref[0]

    xp_scr[base:base + q_in, :] = xbc_ref[0]
    dt_scr[0:q_in, :] = dt_ref[0]

    w = cw_ref[...]
    acc = cb_ref[...] + w[0:1] * xp_scr[pl.ds(base - halo, q), :]
    for k in range(1, CONV_W):
        acc = acc + w[k:k + 1] * xp_scr[pl.ds(base - halo + k, q), :]
    tail = xp_scr[base + l_valid - halo:base + l_valid, :]
    cnew_ref[0] = tail
    xp_scr[base - halo:base, :] = tail

    xa = _silu(acc)
    xs = xa[:, :D_SSM]
    nbc = SSM_GROUPS * SSM_STATE
    bm = xa[:, D_SSM:D_SSM + nbc].astype(BF16)
    cm = xa[:, D_SSM + nbc:].astype(BF16)

    row = lax.broadcasted_iota(jnp.int32, (q, q), 0)
    col = lax.broadcasted_iota(jnp.int32, (q, q), 1)
    causal = row >= col
    tri = causal.astype(BF16)

    dt = _softplus(dt_scr[...] + dtb_ref[...])
    if l_valid < q:
        dt = jnp.where(lax.broadcasted_iota(jnp.int32, dt.shape, 0) < l_valid, dt, 0.0)
    d_a = dt * (-jnp.exp(alog_ref[...]))
    a_hi, a_lo = _split(d_a)
    acum = _dot(tri, a_hi) + _dot(tri, a_lo)
    eye = (lax.broadcasted_iota(jnp.int32, (SSM_HEADS, SSM_HEADS), 0)
           == lax.broadcasted_iota(jnp.int32, (SSM_HEADS, SSM_HEADS), 1)).astype(BF16)
    c_hi, c_lo = _split(acum)
    acum_t = _dot_nt(eye, c_hi) + _dot_nt(eye, c_lo)

    lane_head = lax.broadcasted_iota(jnp.int32, (SSM_HEADS, D_SSM), 1) // SSM_HEAD_DIM
    expand = (lane_head == lax.broadcasted_iota(jnp.int32, (SSM_HEADS, D_SSM), 0)).astype(BF16)

    def ex(v):
        hi, lo = _split(v)
        return _dot(hi, expand) + _dot(lo, expand)

    acum_last = acum[q - 1:q, :]
    xdt = xs * ex(dt)
    xdt_b = xdt.astype(BF16)
    xw_b = (xdt * ex(jnp.exp(acum_last - acum))).astype(BF16)
    dec_in = ex(jnp.exp(acum))
    dec_chunk = ex(jnp.broadcast_to(jnp.exp(acum_last), (8, SSM_HEADS)))[0:1]

    heads_per_group = SSM_HEADS // SSM_GROUPS
    gw = D_SSM // SSM_GROUPS
    h_old = h_scr[...]
    h_old_b = h_old.astype(BF16)
    lane2 = lax.broadcasted_iota(jnp.int32, (q, 2 * SSM_HEAD_DIM), 1)
    y_parts = []
    new_parts = []
    for g in range(SSM_GROUPS):
        bg = bm[:, g * SSM_STATE:(g + 1) * SSM_STATE]
        cg = cm[:, g * SSM_STATE:(g + 1) * SSM_STATE]
        cb = _dot_nt(cg, bg)
        y_off = _dot(cg, h_old_b[:, g * gw:(g + 1) * gw])
        new_parts.append(_dot_tn(bg, xw_b[:, g * gw:(g + 1) * gw]))
        pair_out = []
        for pr in range(heads_per_group // 2):
            lo_l = g * gw + pr * 2 * SSM_HEAD_DIM
            xp2 = xdt_b[:, lo_l:lo_l + 2 * SSM_HEAD_DIM]
            accp = None
            for sub in range(2):
                h = g * heads_per_group + pr * 2 + sub
                seg = acum[:, h:h + 1] - acum_t[h:h + 1, :]
                lmat = jnp.exp(jnp.where(causal, seg, -jnp.inf))
                gmat = (cb * lmat).astype(BF16)
                in_head = (lane2 // SSM_HEAD_DIM) == sub
                rhs = jnp.where(in_head, xp2, jnp.zeros_like(xp2))
                part = _dot(gmat, rhs)
                accp = part if accp is None else accp + part
            pair_out.append(accp)
        y_diag = jnp.concatenate(pair_out, axis=1)
        y_parts.append(y_diag + y_off * dec_in[:, g * gw:(g + 1) * gw])
    y = jnp.concatenate(y_parts, axis=1) + dskip_ref[...] * xs
    h_new = h_old * dec_chunk + jnp.concatenate(new_parts, axis=1)
    h_scr[...] = h_new
    hout_ref[0] = h_new

    gy = y[0:q_in] * _silu(z_ref[0].astype(F32))
    outs = []
    for g in range(SSM_GROUPS):
        sg = gy[:, g * gw:(g + 1) * gw]
        ms = jnp.mean(sg * sg, axis=-1, keepdims=True)
        outs.append(sg * lax.rsqrt(ms + RMS_EPS))
    y_ref[0] = (jnp.concatenate(outs, axis=1) * nw_ref[...]).astype(y_ref.dtype)


def _ssd(xbc, z, dt, conv_st, h0_t, conv_w, conv_b, dt_bias, a_log, d_skip_x, norm_w, q_in, l_valid):
    b, l, _ = xbc.shape
    q = SSD_CHUNK
    nc = l // q_in
    kern = functools.partial(_ssd_kernel, q=q, q_in=q_in, l_valid=l_valid)
    seq = lambda i, c: (i, c, 0)
    per_b = lambda i, c: (i, 0, 0)
    full = lambda i, c: (0, 0)
    return pl.pallas_call(
        kern,
        grid=(b, nc),
        in_specs=[pl.BlockSpec((1, q_in, D_CONV), seq), pl.BlockSpec((1, q_in, D_SSM), seq),
                  pl.BlockSpec((1, q_in, SSM_HEADS), seq),
                  pl.BlockSpec((1, CONV_W - 1, D_CONV), per_b),
                  pl.BlockSpec((1, SSM_STATE, D_SSM), per_b),
                  pl.BlockSpec(conv_w.shape, full), pl.BlockSpec(conv_b.shape, full),
                  pl.BlockSpec(dt_bias.shape, full), pl.BlockSpec(a_log.shape, full),
                  pl.BlockSpec(d_skip_x.shape, full), pl.BlockSpec(norm_w.shape, full)],
        out_specs=[pl.BlockSpec((1, q_in, D_SSM), seq),
                   pl.BlockSpec((1, CONV_W - 1, D_CONV), per_b),
                   pl.BlockSpec((1, SSM_STATE, D_SSM), per_b)],
        out_shape=[jax.ShapeDtypeStruct((b, l, D_SSM), F32),
                   jax.ShapeDtypeStruct((b, CONV_W - 1, D_CONV), F32),
                   jax.ShapeDtypeStruct((b, SSM_STATE, D_SSM), F32)],
        scratch_shapes=[pltpu.VMEM((SSM_STATE, D_SSM), F32),
                        pltpu.VMEM((q + 8, D_CONV), F32),
                        pltpu.VMEM((q, SSM_HEADS), F32)],
        compiler_params=_cparams("parallel", "arbitrary"),
    )(xbc, z, dt, conv_st, h0_t, conv_w, conv_b, dt_bias, a_log, d_skip_x, norm_w)


def _s5_tables(a_re, a_im, log_step, b_re, b_im, c_re, c_im, d, qs):
    g, p = a_re.shape
    h = S5_GROUP_CH
    r = qs * h
    step = jnp.exp(log_step)[:, None]
    k = jnp.arange(qs + 1, dtype=F32)[:, None, None]
    mag = jnp.exp(a_re * step * k)
    ang = a_im * step * k
    lr = mag * jnp.cos(ang)
    li = mag * jnp.sin(ang)
    ab_re, ab_im = lr[1], li[1]
    den = a_re * a_re + a_im * a_im
    num_re = ab_re - 1.0
    cf_re = (num_re * a_re + ab_im * a_im) / den
    cf_im = (ab_im * a_re - num_re * a_im) / den
    bb_re = cf_re[..., None] * b_re - cf_im[..., None] * b_im
    bb_im = cf_re[..., None] * b_im + cf_im[..., None] * b_re
    hp = lax.Precision.HIGHEST
    lb_re = lr[:qs, :, :, None] * bb_re - li[:qs, :, :, None] * bb_im
    lb_im = lr[:qs, :, :, None] * bb_im + li[:qs, :, :, None] * bb_re
    kern = (jnp.einsum('ghp,kgpj->kghj', c_re, lb_re, precision=hp)
            - jnp.einsum('ghp,kgpj->kghj', c_im, lb_im, precision=hp))
    tt = jnp.arange(qs)
    lag = tt[:, None] - tt[None, :]
    m = jnp.where((lag >= 0)[:, :, None, None, None],
                  kern[jnp.clip(lag, 0, qs - 1)], 0.0)
    m = jnp.transpose(m, (2, 0, 3, 1, 4)).reshape(g, r, r)
    rev_re = lr[:qs][::-1]
    rev_im = li[:qs][::-1]
    bp_re = rev_re[..., None] * bb_re - rev_im[..., None] * bb_im
    bp_im = rev_re[..., None] * bb_im + rev_im[..., None] * bb_re
    bp_re = jnp.transpose(bp_re, (1, 2, 0, 3)).reshape(g, p, r)
    bp_im = jnp.transpose(bp_im, (1, 2, 0, 3)).reshape(g, p, r)
    fr = lr[1:, :, None, :]
    fi = li[1:, :, None, :]
    cy_re = c_re[None] * fr - c_im[None] * fi
    cy_im = -(c_re[None] * fi + c_im[None] * fr)
    cy_re = jnp.transpose(cy_re, (1, 0, 2, 3)).reshape(g, r, p)
    cy_im = jnp.transpose(cy_im, (1, 0, 2, 3)).reshape(g, r, p)

    np_ = g // 2
    z_pr = jnp.zeros((np_, p, r), F32)
    e = lambda a: a.reshape(np_, 2, *a.shape[1:])
    bre, bim = e(bp_re), e(bp_im)
    bp = jnp.concatenate([
        jnp.concatenate([bre[:, 0], z_pr], axis=2), jnp.concatenate([z_pr, bre[:, 1]], axis=2),
        jnp.concatenate([bim[:, 0], z_pr], axis=2), jnp.concatenate([z_pr, bim[:, 1]], axis=2)], axis=1)
    z_rp = jnp.zeros((np_, r, p), F32)
    cre, cim = e(cy_re), e(cy_im)
    cp = jnp.concatenate([
        jnp.concatenate([cre[:, 0], z_rp, cim[:, 0], z_rp], axis=2),
        jnp.concatenate([z_rp, cre[:, 1], z_rp, cim[:, 1]], axis=2)], axis=1)
    lam_q = jnp.stack([lr[qs].reshape(np_, 2 * p), li[qs].reshape(np_, 2 * p)], axis=1)
    dcol = jnp.broadcast_to(d.reshape(np_, 2, 1, h), (np_, 2, qs, h)).reshape(np_, 2 * r, 1)
    bp_t = jnp.transpose(bp, (0, 2, 1))
    return e(m).astype(BF16), bp_t.astype(BF16), cp.astype(BF16), lam_q, dcol


def _gelu_tanh(x):
    return 0.5 * x * (1.0 + jnp.tanh(math.sqrt(2.0 / math.pi) * (x + 0.044715 * (x * x * x))))


def _s5_kernel(u_ref, m_ref, bp_ref, cp_ref, lam_ref, d_ref, s0_ref, y_ref, sout_ref,
               loc_re, loc_im, st_re, st_im, *, nb, nc):
    r = m_ref.shape[2]
    pw = lam_ref.shape[2]
    u = u_ref[0].reshape(2 * r, u_ref.shape[-1])
    loc_re[...] = _dot_tn(u, bp_ref[0, :, :pw])
    loc_im[...] = _dot_tn(u, bp_ref[0, :, pw:])
    ar = lam_ref[0, 0:1, :]
    ai = lam_ref[0, 1:2, :]
    sr = s0_ref[0, 0]
    si = s0_ref[0, 1]
    for c in range(nc):
        rows = pl.ds(c, nb, stride=nc) if nc > 1 else pl.ds(0, nb)
        st_re[rows, :] = sr
        st_im[rows, :] = si
        sr, si = (ar * sr - ai * si + loc_re[rows, :], ar * si + ai * sr + loc_im[rows, :])
    sout_ref[0, 0] = sr
    sout_ref[0, 1] = si
    y_off = (_dot_nt(cp_ref[0, :, :pw], st_re[...].astype(BF16))
             + _dot_nt(cp_ref[0, :, pw:], st_im[...].astype(BF16)))
    y_loc = jnp.concatenate([_dot(m_ref[0, 0], u[:r]), _dot(m_ref[0, 1], u[r:])], axis=0)
    y = y_loc + y_off + d_ref[0] * u.astype(F32)
    y_ref[0] = _gelu_tanh(y).astype(y_ref.dtype).reshape(y_ref.shape[1:])


def _s5(ut, s0_re, s0_im, tables, nb):
    m, bp, cp, lam_q, dcol = tables
    np_, _, qs, _, nbc = ut.shape
    nc = nbc // nb
    r = qs * S5_GROUP_CH
    pw = 2 * S5_STATE
    pack = lambda s: jnp.transpose(s.reshape(nb, np_, pw), (1, 0, 2))
    s0 = jnp.stack([pack(s0_re), pack(s0_im)], axis=1)
    kern = functools.partial(_s5_kernel, nb=nb, nc=nc)
    blk = lambda *shape: pl.BlockSpec((1,) + shape, lambda i: (i,) + (0,) * len(shape))
    yt, sout = pl.pallas_call(
        kern,
        grid=(np_,),
        in_specs=[blk(2, qs, S5_GROUP_CH, nbc), blk(2, r, r), blk(2 * r, 2 * pw), blk(2 * r, 2 * pw),
                  blk(2, pw), blk(2 * r, 1), blk(2, nb, pw)],
        out_specs=[blk(2, qs, S5_GROUP_CH, nbc), blk(2, nb, pw)],
        out_shape=[jax.ShapeDtypeStruct(ut.shape, BF16),
                   jax.ShapeDtypeStruct((np_, 2, nb, pw), F32)],
        scratch_shapes=[pltpu.VMEM((nbc, pw), F32)] * 4,
        compiler_params=_cparams("parallel"),
    )(ut, m, bp, cp, lam_q, dcol, s0)
    unpack = lambda s: jnp.transpose(s, (1, 0, 2)).reshape(nb, S5_GROUPS, S5_STATE)
    return yt, unpack(sout[:, 0]), unpack(sout[:, 1])


def _mix_kernel(ym_ref, yt_ref, x_ref, wglu_ref, bglu_ref, wmo_ref, g_ref, b_ref, o_ref):
    tn, slots, _ = x_ref.shape
    qb = yt_ref.shape[2]
    tr = lambda t: yt_ref[:, :, min(t, qb - 1)].reshape(D_S5, tn).astype(F32).T
    yg = jnp.concatenate([tr(t) for t in range(slots)], axis=0)
    ym = jnp.concatenate([ym_ref[:, t, :] for t in range(slots)], axis=0).astype(BF16)
    x = jnp.concatenate([x_ref[:, t, :] for t in range(slots)], axis=0)
    gate = _sigmoid(_dot(yg.astype(BF16), wglu_ref[...]) + bglu_ref[...])
    ys = (yg * gate).astype(BF16)
    mix = _dot(ym, wmo_ref[0:D_SSM, :]) + _dot(ys, wmo_ref[D_SSM:, :])
    out = _layer_norm(ALPHA * x + mix, g_ref[...], b_ref[...])
    for t in range(slots):
        o_ref[:, t, :] = out[t * tn:(t + 1) * tn]


def _mix(ym3, yt, x3, wglu, bglu, wmo, g, b, tn):
    nbc, q, _ = x3.shape
    np_, _, qs, _, _ = yt.shape
    qb = min(qs, SLOT_BLK)
    full = lambda t, j: (0, 0)
    tok = lambda t, j: (j, t, 0)
    return pl.pallas_call(
        _mix_kernel,
        grid=(q // SLOT_BLK, nbc // tn),
        in_specs=[pl.BlockSpec((tn, SLOT_BLK, D_SSM), tok),
                  pl.BlockSpec((np_, 2, qb, S5_GROUP_CH, tn), lambda t, j: (0, 0, jnp.minimum(t, qs // qb - 1), 0, j)),
                  pl.BlockSpec((tn, SLOT_BLK, D_MODEL), tok),
                  pl.BlockSpec(wglu.shape, full), pl.BlockSpec(bglu.shape, full),
                  pl.BlockSpec(wmo.shape, full), pl.BlockSpec(g.shape, full), pl.BlockSpec(b.shape, full)],
        out_specs=pl.BlockSpec((tn, SLOT_BLK, D_MODEL), tok),
        out_shape=jax.ShapeDtypeStruct((nbc, q, D_MODEL), F32),
        compiler_params=_cparams("parallel", "parallel"),
    )(ym3, yt, x3, wglu, bglu, wmo, g, b)


def _memkv_kernel(m_ref, wk_ref, wv_ref, k_ref, v_ref):
    m = m_ref[...].astype(BF16)
    k_ref[...] = _dot(m, wk_ref[...])
    v_ref[...] = _dot(m, wv_ref[...])


def _memkv(mem2d, wk, wv, tm):
    t = mem2d.shape[0]
    full = lambda i: (0, 0)
    row = lambda i: (i, 0)
    return pl.pallas_call(
        _memkv_kernel,
        grid=(t // tm,),
        in_specs=[pl.BlockSpec((tm, D_MODEL), row), pl.BlockSpec(wk.shape, full), pl.BlockSpec(wv.shape, full)],
        out_specs=[pl.BlockSpec((tm, D_MODEL), row), pl.BlockSpec((tm, D_MODEL), row)],
        out_shape=[jax.ShapeDtypeStruct((t, D_MODEL), F32), jax.ShapeDtypeStruct((t, D_MODEL), F32)],
        compiler_params=_cparams("parallel"),
    )(mem2d, wk, wv)


def _attn_kernel(x_ref, k_ref, v_ref, wq_ref, wo_ref, g_ref, b_ref, o_ref):
    x = x_ref[0]
    q = _dot(x.astype(BF16), wq_ref[...]).astype(BF16)
    k = k_ref[0].astype(BF16)
    v = v_ref[0].astype(BF16)
    scale = MEM_HEAD_DIM ** -0.5
    outs = []
    for h in range(MEM_HEADS):
        sl = slice(h * MEM_HEAD_DIM, (h + 1) * MEM_HEAD_DIM)
        s = _dot_nt(q[:, sl], k[:, sl]) * scale
        s = s - jnp.max(s, axis=-1, keepdims=True)
        p = jnp.exp(s)
        p = p / jnp.sum(p, axis=-1, keepdims=True)
        outs.append(_dot(p.astype(BF16), v[:, sl]))
    o = jnp.concatenate(outs, axis=1).astype(BF16)
    att = _dot(o, wo_ref[...])
    o_ref[0] = _layer_norm(ALPHA * x + att, g_ref[...], b_ref[...])


def _attn(x3d, k, v, wq, wo, g, b, tq):
    bsz, l, _ = x3d.shape
    m = k.shape[1]
    full = lambda i, j: (0, 0)
    seq = lambda i, j: (i, j, 0)
    per_b = lambda i, j: (i, 0, 0)
    return pl.pallas_call(
        _attn_kernel,
        grid=(bsz, l // tq),
        in_specs=[pl.BlockSpec((1, tq, D_MODEL), seq),
                  pl.BlockSpec((1, m, D_MODEL), per_b), pl.BlockSpec((1, m, D_MODEL), per_b),
                  pl.BlockSpec(wq.shape, full), pl.BlockSpec(wo.shape, full),
                  pl.BlockSpec(g.shape, full), pl.BlockSpec(b.shape, full)],
        out_specs=pl.BlockSpec((1, tq, D_MODEL), seq),
        out_shape=jax.ShapeDtypeStruct((bsz, l, D_MODEL), F32),
        compiler_params=_cparams("parallel", "parallel"),
    )(x3d, k, v, wq, wo, g, b)


def _router_kernel(x_ref, wh_ref, wl_ref, br_ref, comb_ref, pos_ref, cnt_ref):
    x = x_ref[...]
    tm = x.shape[0]
    xh, xl = _split(x)
    wh = wh_ref[...]
    logits = _dot_nt(wh, xh) + _dot_nt(wh, xl) + _dot_nt(wl_ref[...], xh)
    s = _sigmoid(logits)
    sb = s + br_ref[...]
    per = N_EXPERTS // N_EXPERT_GROUPS
    neg = -jnp.inf

    sb3 = sb.reshape(N_EXPERT_GROUPS, per, tm)
    mem = lax.broadcasted_iota(jnp.int32, sb3.shape, 1)
    m1 = jnp.max(sb3, axis=1, keepdims=True)
    first = jnp.min(jnp.where(sb3 == m1, mem, per), axis=1, keepdims=True)
    m2 = jnp.max(jnp.where(mem == first, neg, sb3), axis=1, keepdims=True)
    gs = (m1 + m2).reshape(N_EXPERT_GROUPS, tm)

    def rank_rows(v):
        n = v.shape[0]
        idx = lax.broadcasted_iota(jnp.int32, v.shape, 0)
        rank = jnp.zeros(v.shape, jnp.int32)
        for j in range(n):
            rj = v[j:j + 1, :]
            beats = (rj > v) | ((rj == v) & (idx > j))
            rank = rank + beats.astype(jnp.int32)
        return rank

    gsel = rank_rows(gs) < TOPK_GROUPS
    emask = jnp.broadcast_to(gsel.reshape(N_EXPERT_GROUPS, 1, tm), sb3.shape).reshape(N_EXPERTS, tm)
    sm = jnp.where(emask, sb, neg)
    sel = (rank_rows(sm) < TOP_K) & emask
    gates = jnp.where(sel, s, 0.0)
    denom = jnp.sum(gates, axis=0, keepdims=True)
    comb_ref[...] = gates / denom * ROUTE_SCALE

    self = jnp.where(sel, 1.0, 0.0)
    before = (lax.broadcasted_iota(jnp.int32, (tm, tm), 0)
              < lax.broadcasted_iota(jnp.int32, (tm, tm), 1)).astype(BF16)
    rank_tok = _dot(self.astype(BF16), before)
    pos_ref[...] = jnp.where(sel, rank_tok.astype(jnp.int32), -1)
    cnt = jnp.sum(self, axis=1, keepdims=True).astype(jnp.int32)
    cnt_ref[0] = jnp.broadcast_to(cnt, cnt_ref.shape[1:])


def _router(x2d, wr_hi_t, wr_lo_t, br_col, tm):
    t = x2d.shape[0]
    nt = t // tm
    full = lambda i: (0, 0)
    return pl.pallas_call(
        _router_kernel,
        grid=(nt,),
        in_specs=[pl.BlockSpec((tm, D_MODEL), lambda i: (i, 0)), pl.BlockSpec(wr_hi_t.shape, full),
                  pl.BlockSpec(wr_lo_t.shape, full), pl.BlockSpec(br_col.shape, full)],
        out_specs=[pl.BlockSpec((N_EXPERTS, tm), lambda i: (0, i)),
                   pl.BlockSpec((N_EXPERTS, tm), lambda i: (0, i)),
                   pl.BlockSpec((1, N_EXPERTS, 128), lambda i: (i, 0, 0))],
        out_shape=[jax.ShapeDtypeStruct((N_EXPERTS, t), F32),
                   jax.ShapeDtypeStruct((N_EXPERTS, t), jnp.int32),
                   jax.ShapeDtypeStruct((nt, N_EXPERTS, 128), jnp.int32)],
        compiler_params=_cparams("parallel"),
    )(x2d, wr_hi_t, wr_lo_t, br_col)


def _moe_rows(t, nt):
    worst = TOP_K * t + nt * N_EXPERTS * (MOE_SEG - 1) + N_EXPERTS * (MOE_TR - 1)
    return -(-worst // MOE_TR) * MOE_TR


def _moe_local_rows(tm):
    worst = TOP_K * tm + N_EXPERTS * (MOE_SEG - 1)
    return -(-worst // 512) * 512


def _moe_plan(cnt, tm):
    nt = cnt.shape[0]
    npad = (cnt + (MOE_SEG - 1)) // MOE_SEG * MOE_SEG
    loc_off = jnp.cumsum(npad, axis=1) - npad
    tot = jnp.sum(npad, axis=0)
    totp = (tot + (MOE_TR - 1)) // MOE_TR * MOE_TR
    gend = jnp.cumsum(totp)
    goff = gend - totp
    seg_off = goff[None, :] + jnp.cumsum(npad, axis=0) - npad
    n_row_tiles = _moe_rows(nt * tm, nt) // MOE_TR
    tile_start = jnp.arange(n_row_tiles, dtype=jnp.int32) * MOE_TR
    ended = (gend[None, :] <= tile_start[:, None]).astype(jnp.int32)
    tile_expert = jnp.minimum(jnp.sum(ended, axis=1), N_EXPERTS - 1)
    n_valid = (gend[-1] // MOE_TR).reshape(1)
    i32 = lambda a: a.astype(jnp.int32)
    return dict(npad=i32(npad.reshape(-1)), loc_off=i32(loc_off.reshape(-1)), seg_off=i32(seg_off.reshape(-1)),
                fill_off=i32(goff + tot), fill_n=i32((totp - tot) // MOE_SEG),
                tile_expert=i32(tile_expert), n_valid=i32(n_valid))


def _for_each_segment(i, npad_ref, loc_ref, seg_ref, body):
    def per_expert(e, n):
        k = i * N_EXPERTS + e
        nch = npad_ref[k] // MOE_SEG
        base = loc_ref[k]
        dst = seg_ref[k]

        def chunk(j, c):
            body(e, j, pl.multiple_of(base + j * MOE_SEG, MOE_SEG), pl.multiple_of(dst + j * MOE_SEG, MOE_SEG))
            return c

        lax.fori_loop(0, nch, chunk, 0)
        return n + nch

    return lax.fori_loop(0, N_EXPERTS, per_expert, 0)


def _one_hot_rows(pos_ref, e, j, tm, value):
    row = pos_ref[pl.ds(e, 1), :]
    want = lax.broadcasted_iota(jnp.int32, (MOE_SEG, tm), 0) + j * MOE_SEG
    return jnp.where(row == want, value, 0.0).astype(BF16)


def _moe_gather_kernel(npad_ref, loc_ref, seg_ref, foff_ref, fn_ref, x_ref, pos_ref, xs_hbm,
                       p_scr, xs_scr, zero_scr, sem):
    i = pl.program_id(0)
    tm = x_ref.shape[0]
    p_scr[...] = jnp.zeros_like(p_scr)

    def put(e, j, lrow, grow):
        p_scr[pl.ds(lrow, MOE_SEG), :] = _one_hot_rows(pos_ref, e, j, tm, 1.0)

    _for_each_segment(i, npad_ref, loc_ref, seg_ref, put)
    xs_scr[...] = _dot(p_scr[...], x_ref[...].astype(BF16)).astype(BF16)

    def send(e, j, lrow, grow):
        pltpu.make_async_copy(xs_scr.at[pl.ds(lrow, MOE_SEG)], xs_hbm.at[pl.ds(grow, MOE_SEG)], sem).start()

    n_sent = _for_each_segment(i, npad_ref, loc_ref, seg_ref, send)

    def wait_one(k, c):
        pltpu.make_async_copy(xs_scr.at[pl.ds(0, MOE_SEG)], xs_hbm.at[pl.ds(0, MOE_SEG)], sem).wait()
        return c

    lax.fori_loop(0, n_sent, wait_one, 0)

    @pl.when(i == pl.num_programs(0) - 1)
    def _():
        zero_scr[...] = jnp.zeros_like(zero_scr)

        def per_expert(e, n):
            def chunk(j, c):
                grow = pl.multiple_of(foff_ref[e] + j * MOE_SEG, MOE_SEG)
                pltpu.make_async_copy(zero_scr, xs_hbm.at[pl.ds(grow, MOE_SEG)], sem).start()
                return c
            lax.fori_loop(0, fn_ref[e], chunk, 0)
            return n + fn_ref[e]

        n_fill = lax.fori_loop(0, N_EXPERTS, per_expert, 0)
        lax.fori_loop(0, n_fill, wait_one, 0)


def _moe_gather(plan, x2d, pos, tm):
    t = x2d.shape[0]
    nt = t // tm
    rows = _moe_rows(t, nt)
    rl = _moe_local_rows(tm)
    return pl.pallas_call(
        _moe_gather_kernel,
        grid_spec=pltpu.PrefetchScalarGridSpec(
            num_scalar_prefetch=5,
            grid=(nt,),
            in_specs=[pl.BlockSpec((tm, D_MODEL), lambda i, *_: (i, 0)),
                      pl.BlockSpec((N_EXPERTS, tm), lambda i, *_: (0, i))],
            out_specs=pl.BlockSpec(memory_space=pl.ANY),
            scratch_shapes=[pltpu.VMEM((rl, tm), BF16), pltpu.VMEM((rl, D_MODEL), BF16),
                            pltpu.VMEM((MOE_SEG, D_MODEL), BF16), pltpu.SemaphoreType.DMA]),
        out_shape=jax.ShapeDtypeStruct((rows, D_MODEL), BF16),
        compiler_params=_cparams("arbitrary"),
    )(plan['npad'], plan['loc_off'], plan['seg_off'], plan['fill_off'], plan['fill_n'], x2d, pos)


def _moe_expert_kernel(te_ref, nv_ref, xs_ref, wg_ref, wu_ref, wd_ref, ys_ref, wg_s, wu_s, wd_s):
    j = pl.program_id(0)
    changed = te_ref[j] != te_ref[jnp.maximum(j - 1, 0)]

    @pl.when((j == 0) | changed)
    def _():
        wg_s[...] = wg_ref[0].astype(BF16)
        wu_s[...] = wu_ref[0].astype(BF16)
        wd_s[...] = wd_ref[0].astype(BF16)

    @pl.when(j < nv_ref[0])
    def _():
        x = xs_ref[...]
        hid = _silu(_dot(x, wg_s[...])) * _dot(x, wu_s[...])
        ys_ref[...] = _dot(hid.astype(BF16), wd_s[...]).astype(ys_ref.dtype)


def _moe_experts(plan, xs, wg, wu, wd):
    rows = xs.shape[0]
    de = wg.shape[2]
    row_tile = lambda j, te, nv: (jnp.minimum(j, nv[0] - 1), 0)
    expert = lambda j, te, nv: (te[jnp.minimum(j, nv[0] - 1)], 0, 0)
    return pl.pallas_call(
        _moe_expert_kernel,
        grid_spec=pltpu.PrefetchScalarGridSpec(
            num_scalar_prefetch=2,
            grid=(rows // MOE_TR,),
            in_specs=[pl.BlockSpec((MOE_TR, D_MODEL), row_tile),
                      pl.BlockSpec((1, D_MODEL, de), expert), pl.BlockSpec((1, D_MODEL, de), expert),
                      pl.BlockSpec((1, de, D_MODEL), expert)],
            out_specs=pl.BlockSpec((MOE_TR, D_MODEL), row_tile),
            scratch_shapes=[pltpu.VMEM((D_MODEL, de), BF16), pltpu.VMEM((D_MODEL, de), BF16),
                            pltpu.VMEM((de, D_MODEL), BF16)]),
        out_shape=jax.ShapeDtypeStruct((rows, D_MODEL), BF16),
        compiler_params=_cparams("arbitrary"),
    )(plan['tile_expert'], plan['n_valid'], xs, wg, wu, wd)


def _moe_combine_kernel(npad_ref, loc_ref, seg_ref, x_ref, pos_ref, comb_ref, ys_hbm, sg_ref, su_ref,
                        sd_ref, g_ref, b_ref, o_ref, pg_scr, ys_scr, sem):
    i = pl.program_id(0)
    tm = x_ref.shape[0]

    @pl.when(i == 0)
    def _():
        ys_scr[...] = jnp.zeros_like(ys_scr)

    def fetch(e, j, lrow, grow):
        pltpu.make_async_copy(ys_hbm.at[pl.ds(grow, MOE_SEG)], ys_scr.at[pl.ds(lrow, MOE_SEG)], sem).start()

    n_fetch = _for_each_segment(i, npad_ref, loc_ref, seg_ref, fetch)

    pg_scr[...] = jnp.zeros_like(pg_scr)

    def put(e, j, lrow, grow):
        pg_scr[pl.ds(lrow, MOE_SEG), :] = _one_hot_rows(pos_ref, e, j, tm, comb_ref[pl.ds(e, 1), :])

    _for_each_segment(i, npad_ref, loc_ref, seg_ref, put)

    x = x_ref[...]
    xb = x.astype(BF16)
    hs = _silu(_dot(xb, sg_ref[...])) * _dot(xb, su_ref[...])
    shared = _dot(hs.astype(BF16), sd_ref[...])

    def wait_one(k, c):
        pltpu.make_async_copy(ys_hbm.at[pl.ds(0, MOE_SEG)], ys_scr.at[pl.ds(0, MOE_SEG)], sem).wait()
        return c

    lax.fori_loop(0, n_fetch, wait_one, 0)
    routed = _dot_tn(pg_scr[...], ys_scr[...])
    o_ref[...] = _layer_norm(ALPHA * x + (routed + shared), g_ref[...], b_ref[...])


def _moe_combine(plan, x2d, pos, comb, ys, sg, su, sd, g, b, tm):
    t = x2d.shape[0]
    nt = t // tm
    rl = _moe_local_rows(tm)
    full = lambda i, *_: (0, 0)
    return pl.pallas_call(
        _moe_combine_kernel,
        grid_spec=pltpu.PrefetchScalarGridSpec(
            num_scalar_prefetch=3,
            grid=(nt,),
            in_specs=[pl.BlockSpec((tm, D_MODEL), lambda i, *_: (i, 0)),
                      pl.BlockSpec((N_EXPERTS, tm), lambda i, *_: (0, i)),
                      pl.BlockSpec((N_EXPERTS, tm), lambda i, *_: (0, i)),
                      pl.BlockSpec(memory_space=pl.ANY),
                      pl.BlockSpec(sg.shape, full), pl.BlockSpec(su.shape, full), pl.BlockSpec(sd.shape, full),
                      pl.BlockSpec(g.shape, full), pl.BlockSpec(b.shape, full)],
            out_specs=pl.BlockSpec((tm, D_MODEL), lambda i, *_: (i, 0)),
            scratch_shapes=[pltpu.VMEM((rl, tm), BF16), pltpu.VMEM((rl, D_MODEL), BF16),
                            pltpu.SemaphoreType.DMA]),
        out_shape=jax.ShapeDtypeStruct((t, D_MODEL), F32),
        compiler_params=_cparams("arbitrary"),
    )(plan['npad'], plan['loc_off'], plan['seg_off'], x2d, pos, comb, ys, sg, su, sd, g, b)


def _moe(x2d, w, tm):
    comb, pos, cnt = _router(x2d, w['wr_hi_t'], w['wr_lo_t'], w['br_col'], tm)
    plan = _moe_plan(cnt[:, :, 0], tm)
    xs = _moe_gather(plan, x2d, pos, tm)
    ys = _moe_experts(plan, xs, w['w_e_gate'], w['w_e_up'], w['w_e_down'])
    return _moe_combine(plan, x2d, pos, comb, ys, w['w_s_gate'], w['w_s_up'], w['w_s_down'],
                        w['ln3_g'], w['ln3_b'], tm)


def _layer(x3d, mem_k, mem_v, conv_st, ssm_st, s5_re, s5_im, w, *, l_valid, s5_chunk, tm, tq):
    bsz, lp, _ = x3d.shape
    t = bsz * lp
    x2d = x3d.reshape(t, D_MODEL)
    tm = min(tm, t)
    tq = min(tq, lp)
    z, xbc, dt = _in_proj(x2d, w['wz'], w['wx'], w['wd'], tm)
    q_in = min(lp, SSD_CHUNK)
    h0_t = jnp.transpose(ssm_st.reshape(bsz, D_SSM, SSM_STATE), (0, 2, 1))
    ym, conv_new, h_t = _ssd(xbc.reshape(bsz, lp, D_CONV), z.reshape(bsz, lp, D_SSM),
                             dt.reshape(bsz, lp, SSM_HEADS), conv_st, h0_t,
                             w['conv_w'], w['conv_b'], w['dt_bias'], w['a_log'], w['d_skip_x'],
                             w['norm_ssm'], q_in, min(l_valid, q_in))
    ssm_new = jnp.transpose(h_t, (0, 2, 1)).reshape(bsz, SSM_HEADS, SSM_HEAD_DIM, SSM_STATE)
    n_chunks = l_valid // s5_chunk
    nbc = bsz * n_chunks
    slots = lp // n_chunks
    tn = min(SLOT_TN, nbc)
    x3 = x3d.reshape(nbc, slots, D_MODEL)
    ut = _in_proj_u(x3, w['wu_t'], s5_chunk, tn)
    yt, s5_re_new, s5_im_new = _s5(ut, s5_re, s5_im, w['s5_tables'][s5_chunk], bsz)
    x1 = _mix(ym.reshape(nbc, slots, D_SSM), yt, x3, w['w_glu'], w['b_glu'], w['w_mix_out'],
              w['ln1_g'], w['ln1_b'], tn)
    x2 = _attn(x1.reshape(bsz, lp, D_MODEL), mem_k, mem_v, w['w_q'], w['w_o'], w['ln2_g'], w['ln2_b'], tq)
    return x2.reshape(t, D_MODEL), conv_new, ssm_new, s5_re_new, s5_im_new


def kernel(x_prompt, x_sample, mem_prompt, state_conv, state_ssm, state_s5_re, state_s5_im, cache_mem_k, cache_mem_v, w_in, conv_w, conv_b, dt_bias, a_log, d_ssm, norm_ssm, s5_a_re, s5_a_im, s5_log_step, s5_b_re, s5_b_im, s5_c_re, s5_c_im, s5_d, w_glu, b_glu, w_mix_out, ln1_g, ln1_b, w_q, w_k, w_v, w_o, ln2_g, ln2_b, w_router, b_router, w_e_gate, w_e_up, w_e_down, w_s_gate, w_s_up, w_s_down, ln3_g, ln3_b):
    assert w_in.shape[0] == DEPTH
    bp, lp, _ = x_prompt.shape
    bs, ls, _ = x_sample.shape
    row = lambda a: a.reshape(1, -1)
    bf = lambda a: a.astype(BF16)
    w_in0 = w_in[0]
    o1 = D_SSM
    o2 = o1 + D_CONV
    o3 = o2 + SSM_HEADS
    wr_hi, wr_lo = _split(w_router[0].T)
    s5_args = (s5_a_re[0], s5_a_im[0], s5_log_step[0], s5_b_re[0], s5_b_im[0], s5_c_re[0], s5_c_im[0], s5_d[0])
    s5_chunk_p = math.gcd(lp, 16)
    w = {
        'wz': bf(w_in0[:, :o1]), 'wx': bf(w_in0[:, o1:o2]), 'wd': bf(w_in0[:, o2:o3]), 'wu_t': bf(w_in0[:, o3:].T),
        'conv_w': conv_w[0], 'conv_b': row(conv_b[0]), 'dt_bias': row(dt_bias[0]), 'a_log': row(a_log[0]),
        'd_skip_x': row(jnp.repeat(d_ssm[0], SSM_HEAD_DIM)), 'norm_ssm': row(norm_ssm[0]),
        's5_tables': {q: _s5_tables(*s5_args, q) for q in sorted({s5_chunk_p, ls})},
        'w_glu': bf(w_glu[0]), 'b_glu': row(b_glu[0]), 'w_mix_out': bf(w_mix_out[0]),
        'ln1_g': row(ln1_g[0]), 'ln1_b': row(ln1_b[0]),
        'w_q': bf(w_q[0]), 'w_o': bf(w_o[0]), 'ln2_g': row(ln2_g[0]), 'ln2_b': row(ln2_b[0]),
        'wr_hi_t': wr_hi, 'wr_lo_t': wr_lo, 'br_col': b_router[0].reshape(-1, 1),
        'w_e_gate': w_e_gate[0], 'w_e_up': w_e_up[0], 'w_e_down': w_e_down[0],
        'w_s_gate': bf(w_s_gate[0]), 'w_s_up': bf(w_s_up[0]), 'w_s_down': bf(w_s_down[0]),
        'ln3_g': row(ln3_g[0]), 'ln3_b': row(ln3_b[0]),
    }

    m_tok = mem_prompt.shape[1]
    mk2d, mv2d = _memkv(mem_prompt.reshape(bp * m_tok, D_MODEL), bf(w_k[0]), bf(w_v[0]), 512)
    mk = mk2d.reshape(bp, m_tok, D_MODEL)
    mv = mv2d.reshape(bp, m_tok, D_MODEL)
    zeros = lambda *s: jnp.zeros(s, F32)
    x2p, conv_p, ssm_p, s5r_p, s5i_p = _layer(
        x_prompt, mk, mv, zeros(bp, CONV_W - 1, D_CONV), zeros(bp, SSM_HEADS, SSM_HEAD_DIM, SSM_STATE),
        zeros(bp, S5_GROUPS, S5_STATE), zeros(bp, S5_GROUPS, S5_STATE), w,
        l_valid=lp, s5_chunk=s5_chunk_p, tm=512, tq=512)

    xs_pad = jnp.pad(x_sample, ((0, 0), (0, SAMPLE_PAD - ls), (0, 0)))
    ck = cache_mem_k[0].reshape(bs, m_tok, D_MODEL)
    cv = cache_mem_v[0].reshape(bs, m_tok, D_MODEL)
    x2s, conv_s, ssm_s, s5r_s, s5i_s = _layer(
        xs_pad, ck, cv, state_conv[0], state_ssm[0], state_s5_re[0], state_s5_im[0], w,
        l_valid=ls, s5_chunk=ls, tm=512, tq=SAMPLE_PAD)

    tp = bp * lp
    y_all = _moe(jnp.concatenate([x2p, x2s], axis=0), w, MOE_TM)
    yp = y_all[:tp].reshape(bp, lp, D_MODEL)
    ys = y_all[tp:].reshape(bs, SAMPLE_PAD, D_MODEL)[:, :ls]

    kv_shape = (1, bp, m_tok, MEM_HEADS, MEM_HEAD_DIM)
    return (yp, ys, conv_p[None], ssm_p[None], s5r_p[None], s5i_p[None],
            mk.reshape(kv_shape), mv.reshape(kv_shape),
            conv_s[None], ssm_s[None], s5r_s[None], s5i_s[None])
```

```python
import functools
import math

import jax
import jax.numpy as jnp
from jax import lax
from jax.experimental import pallas as pl
from jax.experimental.pallas import tpu as pltpu

F32 = jnp.float32
BF16 = jnp.bfloat16

D_MODEL = 1024
D_SSM = 1024
SSM_HEAD_DIM = 64
SSM_HEADS = D_SSM // SSM_HEAD_DIM
SSM_GROUPS = 2
SSM_STATE = 128
CONV_W = 4
D_CONV = D_SSM + 2 * SSM_GROUPS * SSM_STATE
D_S5 = 1024
S5_GROUP_CH = 16
S5_GROUPS = D_S5 // S5_GROUP_CH
S5_STATE = 64
MEM_HEADS = 4
MEM_HEAD_DIM = D_MODEL // MEM_HEADS
N_EXPERTS = 64
N_EXPERT_GROUPS = 8
TOPK_GROUPS = 4
TOP_K = 6
ROUTE_SCALE = 2.5
DEPTH = 1
ALPHA = (2.0 * DEPTH) ** 0.25
LN_EPS = 1e-5
RMS_EPS = 1e-5

SSD_CHUNK = 128
SAMPLE_PAD = 8
SLOT_BLK = 8
SLOT_TN = 128
ATTN_CACHE_NB = 4
MOE_TM = 256
MOE_TR = 512
MOE_SEG = 16
VMEM_LIMIT = 48 * 1024 * 1024


def _cparams(*sem):
    return pltpu.CompilerParams(dimension_semantics=sem, vmem_limit_bytes=VMEM_LIMIT)


def _dot(a, b):
    return jnp.dot(a, b, preferred_element_type=F32)


def _dot_nt(a, b):
    return lax.dot_general(a, b, (((1,), (1,)), ((), ())), preferred_element_type=F32)


def _dot_tn(a, b):
    return lax.dot_general(a, b, (((0,), (0,)), ((), ())), preferred_element_type=F32)


def _split(v):
    hi = v.astype(BF16)
    lo = (v - hi.astype(F32)).astype(BF16)
    return hi, lo


def _silu(x):
    return x * (1.0 / (1.0 + jnp.exp(-x)))


def _sigmoid(x):
    return 1.0 / (1.0 + jnp.exp(-x))


def _softplus(x):
    return jnp.maximum(x, 0.0) + jnp.log(1.0 + jnp.exp(-jnp.abs(x)))


def _layer_norm(x, g, b):
    mu = jnp.mean(x, axis=-1, keepdims=True)
    xc = x - mu
    var = jnp.mean(xc * xc, axis=-1, keepdims=True)
    return xc * lax.rsqrt(var + LN_EPS) * g + b


def _in_proj_kernel(x_ref, wz_ref, wx_ref, wd_ref, z_ref, xbc_ref, dt_ref):
    x = x_ref[...].astype(BF16)
    z_ref[...] = _dot(x, wz_ref[...]).astype(z_ref.dtype)
    xbc_ref[...] = _dot(x, wx_ref[...])
    dt_ref[...] = _dot(x, wd_ref[...])


def _in_proj(x2d, wz, wx, wd, tm):
    t = x2d.shape[0]
    full = lambda i: (0, 0)
    row = lambda i: (i, 0)
    return pl.pallas_call(
        _in_proj_kernel,
        grid=(t // tm,),
        in_specs=[pl.BlockSpec((tm, D_MODEL), row),
                  pl.BlockSpec(wz.shape, full), pl.BlockSpec(wx.shape, full), pl.BlockSpec(wd.shape, full)],
        out_specs=[pl.BlockSpec((tm, D_SSM), row), pl.BlockSpec((tm, D_CONV), row),
                   pl.BlockSpec((tm, SSM_HEADS), row)],
        out_shape=[jax.ShapeDtypeStruct((t, D_SSM), BF16), jax.ShapeDtypeStruct((t, D_CONV), F32),
                   jax.ShapeDtypeStruct((t, SSM_HEADS), F32)],
        compiler_params=_cparams("parallel"),
    )(x2d, wz, wx, wd)


def _in_proj_u_kernel(x_ref, wu_ref, u_ref):
    np_, _, qb, h, tn = u_ref.shape
    for t in range(qb):
        ut = _dot_nt(wu_ref[...], x_ref[:, t, :].astype(BF16))
        u_ref[:, :, t] = ut.astype(u_ref.dtype).reshape(np_, 2, h, tn)


def _in_proj_u(x3, wu_t, qs, tn):
    nbc = x3.shape[0]
    np_ = S5_GROUPS // 2
    qb = min(qs, SLOT_BLK)
    return pl.pallas_call(
        _in_proj_u_kernel,
        grid=(qs // qb, nbc // tn),
        in_specs=[pl.BlockSpec((tn, SLOT_BLK, D_MODEL), lambda t, j: (j, t, 0)),
                  pl.BlockSpec(wu_t.shape, lambda t, j: (0, 0))],
        out_specs=pl.BlockSpec((np_, 2, qb, S5_GROUP_CH, tn), lambda t, j: (0, 0, t, 0, j)),
        out_shape=jax.ShapeDtypeStruct((np_, 2, qs, S5_GROUP_CH, nbc), BF16),
        compiler_params=_cparams("parallel", "parallel"),
    )(x3, wu_t)


def _ssd_kernel(xbc_ref, z_ref, dt_ref, cst_ref, h0_ref, cw_ref, cb_ref, dtb_ref, alog_ref,
                dskip_ref, nw_ref, y_ref, cnew_ref, hout_ref, h_scr, xp_scr, dt_scr,
                *, q, q_in, l_valid):
    c = pl.program_id(1)
    halo = CONV_W - 1
    base = 8

    @pl.when(c == 0)
    def _():
        h_scr[...] = h0_ref[0]
        xp_scr[...] = jnp.zeros_like(xp_scr)
        dt_scr[...] = jnp.zeros_like(dt_scr)
        xp_scr[base - halo:base, :] = cst_ref[0]

    xp_scr[base:base + q_in, :] = xbc_ref[0]
    dt_scr[0:q_in, :] = dt_ref[0]

    w = cw_ref[...]
    acc = cb_ref[...] + w[0:1] * xp_scr[pl.ds(base - halo, q), :]
    for k in range(1, CONV_W):
        acc = acc + w[k:k + 1] * xp_scr[pl.ds(base - halo + k, q), :]
    tail = xp_scr[base + l_valid - halo:base + l_valid, :]
    cnew_ref[0] = tail
    xp_scr[base - halo:base, :] = tail

    xa = _silu(acc)
    xs = xa[:, :D_SSM]
    nbc = SSM_GROUPS * SSM_STATE
    bm = xa[:, D_SSM:D_SSM + nbc].astype(BF16)
    cm = xa[:, D_SSM + nbc:].astype(BF16)

    row = lax.broadcasted_iota(jnp.int32, (q, q), 0)
    col = lax.broadcasted_iota(jnp.int32, (q, q), 1)
    causal = row >= col
    tri = causal.astype(BF16)

    dt = _softplus(dt_scr[...] + dtb_ref[...])
    if l_valid < q:
        dt = jnp.where(lax.broadcasted_iota(jnp.int32, dt.shape, 0) < l_valid, dt, 0.0)
    d_a = dt * (-jnp.exp(alog_ref[...]))
    a_hi, a_lo = _split(d_a)
    acum = _dot(tri, a_hi) + _dot(tri, a_lo)
    eye = (lax.broadcasted_iota(jnp.int32, (SSM_HEADS, SSM_HEADS), 0)
           == lax.broadcasted_iota(jnp.int32, (SSM_HEADS, SSM_HEADS), 1)).astype(BF16)
    c_hi, c_lo = _split(acum)
    acum_t = _dot_nt(eye, c_hi) + _dot_nt(eye, c_lo)

    lane_head = lax.broadcasted_iota(jnp.int32, (SSM_HEADS, D_SSM), 1) // SSM_HEAD_DIM
    expand = (lane_head == lax.broadcasted_iota(jnp.int32, (SSM_HEADS, D_SSM), 0)).astype(BF16)

    def ex(v):
        hi, lo = _split(v)
        return _dot(hi, expand) + _dot(lo, expand)

    acum_last = acum[q - 1:q, :]
    xdt = xs * ex(dt)
    xdt_b = xdt.astype(BF16)
    xw_b = (xdt * ex(jnp.exp(acum_last - acum))).astype(BF16)
    dec_in = ex(jnp.exp(acum))
    dec_chunk = ex(jnp.broadcast_to(jnp.exp(acum_last), (8, SSM_HEADS)))[0:1]

    heads_per_group = SSM_HEADS // SSM_GROUPS
    gw = D_SSM // SSM_GROUPS
    h_old = h_scr[...]
    h_old_b = h_old.astype(BF16)
    lane2 = lax.broadcasted_iota(jnp.int32, (q, 2 * SSM_HEAD_DIM), 1)
    y_parts = []
    new_parts = []
    for g in range(SSM_GROUPS):
        bg = bm[:, g * SSM_STATE:(g + 1) * SSM_STATE]
        cg = cm[:, g * SSM_STATE:(g + 1) * SSM_STATE]
        cb = _dot_nt(cg, bg)
        y_off = _dot(cg, h_old_b[:, g * gw:(g + 1) * gw])
        new_parts.append(_dot_tn(bg, xw_b[:, g * gw:(g + 1) * gw]))
        pair_out = []
        for pr in range(heads_per_group // 2):
            lo_l = g * gw + pr * 2 * SSM_HEAD_DIM
            xp2 = xdt_b[:, lo_l:lo_l + 2 * SSM_HEAD_DIM]
            accp = None
            for sub in range(2):
                h = g * heads_per_group + pr * 2 + sub
                seg = acum[:, h:h + 1] - acum_t[h:h + 1, :]
                lmat = jnp.exp(jnp.where(causal, seg, -jnp.inf))
                gmat = (cb * lmat).astype(BF16)
                in_head = (lane2 // SSM_HEAD_DIM) == sub
                rhs = jnp.where(in_head, xp2, jnp.zeros_like(xp2))
                part = _dot(gmat, rhs)
                accp = part if accp is None else accp + part
            pair_out.append(accp)
        y_diag = jnp.concatenate(pair_out, axis=1)
        y_parts.append(y_diag + y_off * dec_in[:, g * gw:(g + 1) * gw])
    y = jnp.concatenate(y_parts, axis=1) + dskip_ref[...] * xs
    h_new = h_old * dec_chunk + jnp.concatenate(new_parts, axis=1)
    h_scr[...] = h_new
    hout_ref[0] = h_new

    gy = y[0:q_in] * _silu(z_ref[0].astype(F32))
    outs = []
    for g in range(SSM_GROUPS):
        sg = gy[:, g * gw:(g + 1) * gw]
        ms = jnp.mean(sg * sg, axis=-1, keepdims=True)
        outs.append(sg * lax.rsqrt(ms + RMS_EPS))
    y_ref[0] = (jnp.concatenate(outs, axis=1) * nw_ref[...]).astype(y_ref.dtype)


def _ssd(xbc, z, dt, conv_st, h0_t, conv_w, conv_b, dt_bias, a_log, d_skip_x, norm_w, q_in, l_valid):
    b, l, _ = xbc.shape
    q = SSD_CHUNK
    nc = l // q_in
    kern = functools.partial(_ssd_kernel, q=q, q_in=q_in, l_valid=l_valid)
    seq = lambda i, c: (i, c, 0)
    per_b = lambda i, c: (i, 0, 0)
    full = lambda i, c: (0, 0)
    return pl.pallas_call(
        kern,
        grid=(b, nc),
        in_specs=[pl.BlockSpec((1, q_in, D_CONV), seq), pl.BlockSpec((1, q_in, D_SSM), seq),
                  pl.BlockSpec((1, q_in, SSM_HEADS), seq),
                  pl.BlockSpec((1, CONV_W - 1, D_CONV), per_b),
                  pl.BlockSpec((1, SSM_STATE, D_SSM), per_b),
                  pl.BlockSpec(conv_w.shape, full), pl.BlockSpec(conv_b.shape, full),
                  pl.BlockSpec(dt_bias.shape, full), pl.BlockSpec(a_log.shape, full),
                  pl.BlockSpec(d_skip_x.shape, full), pl.BlockSpec(norm_w.shape, full)],
        out_specs=[pl.BlockSpec((1, q_in, D_SSM), seq),
                   pl.BlockSpec((1, CONV_W - 1, D_CONV), per_b),
                   pl.BlockSpec((1, SSM_STATE, D_SSM), per_b)],
        out_shape=[jax.ShapeDtypeStruct((b, l, D_SSM), F32),
                   jax.ShapeDtypeStruct((b, CONV_W - 1, D_CONV), F32),
                   jax.ShapeDtypeStruct((b, SSM_STATE, D_SSM), F32)],
        scratch_shapes=[pltpu.VMEM((SSM_STATE, D_SSM), F32),
                        pltpu.VMEM((q + 8, D_CONV), F32),
                        pltpu.VMEM((q, SSM_HEADS), F32)],
        compiler_params=_cparams("parallel", "arbitrary"),
    )(xbc, z, dt, conv_st, h0_t, conv_w, conv_b, dt_bias, a_log, d_skip_x, norm_w)


def _s5_tables(a_re, a_im, log_step, b_re, b_im, c_re, c_im, d, qs):
    g, p = a_re.shape
    h = S5_GROUP_CH
    r = qs * h
    step = jnp.exp(log_step)[:, None]
    k = jnp.arange(qs + 1, dtype=F32)[:, None, None]
    mag = jnp.exp(a_re * step * k)
    ang = a_im * step * k
    lr = mag * jnp.cos(ang)
    li = mag * jnp.sin(ang)
    ab_re, ab_im = lr[1], li[1]
    den = a_re * a_re + a_im * a_im
    num_re = ab_re - 1.0
    cf_re = (num_re * a_re + ab_im * a_im) / den
    cf_im = (ab_im * a_re - num_re * a_im) / den
    bb_re = cf_re[..., None] * b_re - cf_im[..., None] * b_im
    bb_im = cf_re[..., None] * b_im + cf_im[..., None] * b_re
    hp = lax.Precision.HIGHEST
    lb_re = lr[:qs, :, :, None] * bb_re - li[:qs, :, :, None] * bb_im
    lb_im = lr[:qs, :, :, None] * bb_im + li[:qs, :, :, None] * bb_re
    kern = (jnp.einsum('ghp,kgpj->kghj', c_re, lb_re, precision=hp)
            - jnp.einsum('ghp,kgpj->kghj', c_im, lb_im, precision=hp))
    none = jnp.zeros_like(kern[0])
    m = jnp.stack([jnp.stack([kern[tp - t] if t <= tp else none for t in range(qs)], axis=2)
                   for tp in range(qs)], axis=1).reshape(g, r, r)
    rev_re = lr[:qs][::-1]
    rev_im = li[:qs][::-1]
    bp_re = rev_re[..., None] * bb_re - rev_im[..., None] * bb_im
    bp_im = rev_re[..., None] * bb_im + rev_im[..., None] * bb_re
    bp_re = jnp.transpose(bp_re, (1, 2, 0, 3)).reshape(g, p, r)
    bp_im = jnp.transpose(bp_im, (1, 2, 0, 3)).reshape(g, p, r)
    fr = lr[1:, :, None, :]
    fi = li[1:, :, None, :]
    cy_re = c_re[None] * fr - c_im[None] * fi
    cy_im = -(c_re[None] * fi + c_im[None] * fr)
    cy_re = jnp.transpose(cy_re, (1, 0, 2, 3)).reshape(g, r, p)
    cy_im = jnp.transpose(cy_im, (1, 0, 2, 3)).reshape(g, r, p)

    np_ = g // 2
    z_pr = jnp.zeros((np_, p, r), F32)
    e = lambda a: a.reshape(np_, 2, *a.shape[1:])
    bre, bim = e(bp_re), e(bp_im)
    bp = jnp.concatenate([
        jnp.concatenate([bre[:, 0], z_pr], axis=2), jnp.concatenate([z_pr, bre[:, 1]], axis=2),
        jnp.concatenate([bim[:, 0], z_pr], axis=2), jnp.concatenate([z_pr, bim[:, 1]], axis=2)], axis=1)
    z_rp = jnp.zeros((np_, r, p), F32)
    cre, cim = e(cy_re), e(cy_im)
    cp = jnp.concatenate([
        jnp.concatenate([cre[:, 0], z_rp, cim[:, 0], z_rp], axis=2),
        jnp.concatenate([z_rp, cre[:, 1], z_rp, cim[:, 1]], axis=2)], axis=1)
    lam_q = jnp.stack([lr[qs].reshape(np_, 2 * p), li[qs].reshape(np_, 2 * p)], axis=1)
    dcol = jnp.broadcast_to(d.reshape(np_, 2, 1, h), (np_, 2, qs, h)).reshape(np_, 2 * r, 1)
    bp_t = jnp.transpose(bp, (0, 2, 1))
    return e(m).astype(BF16), bp_t.astype(BF16), cp.astype(BF16), lam_q, dcol


def _gelu_tanh(x):
    return 0.5 * x * (1.0 + jnp.tanh(math.sqrt(2.0 / math.pi) * (x + 0.044715 * (x * x * x))))


def _s5_kernel(u_ref, m_ref, bp_ref, cp_ref, lam_ref, d_ref, s0_ref, y_ref, sout_ref,
               loc_re, loc_im, st_re, st_im, *, nb, nc):
    r = m_ref.shape[2]
    pw = lam_ref.shape[2]
    u = u_ref[0].reshape(2 * r, u_ref.shape[-1])
    loc_re[...] = _dot_tn(u, bp_ref[0, :, :pw])
    loc_im[...] = _dot_tn(u, bp_ref[0, :, pw:])
    ar = lam_ref[0, 0:1, :]
    ai = lam_ref[0, 1:2, :]
    sr = s0_ref[0, 0]
    si = s0_ref[0, 1]
    for c in range(nc):
        rows = pl.ds(c, nb, stride=nc) if nc > 1 else pl.ds(0, nb)
        st_re[rows, :] = sr
        st_im[rows, :] = si
        sr, si = (ar * sr - ai * si + loc_re[rows, :], ar * si + ai * sr + loc_im[rows, :])
    sout_ref[0, 0] = sr
    sout_ref[0, 1] = si
    y_off = (_dot_nt(cp_ref[0, :, :pw], st_re[...].astype(BF16))
             + _dot_nt(cp_ref[0, :, pw:], st_im[...].astype(BF16)))
    y_loc = jnp.concatenate([_dot(m_ref[0, 0], u[:r]), _dot(m_ref[0, 1], u[r:])], axis=0)
    y = y_loc + y_off + d_ref[0] * u.astype(F32)
    y_ref[0] = _gelu_tanh(y).astype(y_ref.dtype).reshape(y_ref.shape[1:])


def _s5(ut, s0_re, s0_im, tables, nb):
    m, bp, cp, lam_q, dcol = tables
    np_, _, qs, _, nbc = ut.shape
    nc = nbc // nb
    r = qs * S5_GROUP_CH
    pw = 2 * S5_STATE
    pack = lambda s: jnp.transpose(s.reshape(nb, np_, pw), (1, 0, 2))
    s0 = jnp.stack([pack(s0_re), pack(s0_im)], axis=1)
    kern = functools.partial(_s5_kernel, nb=nb, nc=nc)
    blk = lambda *shape: pl.BlockSpec((1,) + shape, lambda i: (i,) + (0,) * len(shape))
    yt, sout = pl.pallas_call(
        kern,
        grid=(np_,),
        in_specs=[blk(2, qs, S5_GROUP_CH, nbc), blk(2, r, r), blk(2 * r, 2 * pw), blk(2 * r, 2 * pw),
                  blk(2, pw), blk(2 * r, 1), blk(2, nb, pw)],
        out_specs=[blk(2, qs, S5_GROUP_CH, nbc), blk(2, nb, pw)],
        out_shape=[jax.ShapeDtypeStruct(ut.shape, BF16),
                   jax.ShapeDtypeStruct((np_, 2, nb, pw), F32)],
        scratch_shapes=[pltpu.VMEM((nbc, pw), F32)] * 4,
        compiler_params=_cparams("parallel"),
    )(ut, m, bp, cp, lam_q, dcol, s0)
    unpack = lambda s: jnp.transpose(s, (1, 0, 2)).reshape(nb, S5_GROUPS, S5_STATE)
    return yt, unpack(sout[:, 0]), unpack(sout[:, 1])


def _mix_kernel(ym_ref, yt_ref, x_ref, wglu_ref, bglu_ref, wmo_ref, g_ref, b_ref, o_ref):
    tn, slots, _ = x_ref.shape
    qb = yt_ref.shape[2]
    tr = lambda t: yt_ref[:, :, min(t, qb - 1)].reshape(D_S5, tn).astype(F32).T
    yg = jnp.concatenate([tr(t) for t in range(slots)], axis=0)
    ym = jnp.concatenate([ym_ref[:, t, :] for t in range(slots)], axis=0).astype(BF16)
    x = jnp.concatenate([x_ref[:, t, :] for t in range(slots)], axis=0)
    gate = _sigmoid(_dot(yg.astype(BF16), wglu_ref[...]) + bglu_ref[...])
    ys = (yg * gate).astype(BF16)
    mix = _dot(ym, wmo_ref[0:D_SSM, :]) + _dot(ys, wmo_ref[D_SSM:, :])
    out = _layer_norm(ALPHA * x + mix, g_ref[...], b_ref[...])
    for t in range(slots):
        o_ref[:, t, :] = out[t * tn:(t + 1) * tn]


def _mix(ym3, yt, x3, wglu, bglu, wmo, g, b, tn):
    nbc, q, _ = x3.shape
    np_, _, qs, _, _ = yt.shape
    qb = min(qs, SLOT_BLK)
    full = lambda t, j: (0, 0)
    tok = lambda t, j: (j, t, 0)
    return pl.pallas_call(
        _mix_kernel,
        grid=(q // SLOT_BLK, nbc // tn),
        in_specs=[pl.BlockSpec((tn, SLOT_BLK, D_SSM), tok),
                  pl.BlockSpec((np_, 2, qb, S5_GROUP_CH, tn), lambda t, j: (0, 0, jnp.minimum(t, qs // qb - 1), 0, j)),
                  pl.BlockSpec((tn, SLOT_BLK, D_MODEL), tok),
                  pl.BlockSpec(wglu.shape, full), pl.BlockSpec(bglu.shape, full),
                  pl.BlockSpec(wmo.shape, full), pl.BlockSpec(g.shape, full), pl.BlockSpec(b.shape, full)],
        out_specs=pl.BlockSpec((tn, SLOT_BLK, D_MODEL), tok),
        out_shape=jax.ShapeDtypeStruct((nbc, q, D_MODEL), F32),
        compiler_params=_cparams("parallel", "parallel"),
    )(ym3, yt, x3, wglu, bglu, wmo, g, b)


def _memkv_kernel(m_ref, wk_ref, wv_ref, k_ref, v_ref):
    m = m_ref[...].astype(BF16)
    k_ref[...] = _dot(m, wk_ref[...])
    v_ref[...] = _dot(m, wv_ref[...])


def _memkv(mem2d, wk, wv, tm):
    t = mem2d.shape[0]
    full = lambda i: (0, 0)
    row = lambda i: (i, 0)
    return pl.pallas_call(
        _memkv_kernel,
        grid=(t // tm,),
        in_specs=[pl.BlockSpec((tm, D_MODEL), row), pl.BlockSpec(wk.shape, full), pl.BlockSpec(wv.shape, full)],
        out_specs=[pl.BlockSpec((tm, D_MODEL), row), pl.BlockSpec((tm, D_MODEL), row)],
        out_shape=[jax.ShapeDtypeStruct((t, D_MODEL), F32), jax.ShapeDtypeStruct((t, D_MODEL), F32)],
        compiler_params=_cparams("parallel"),
    )(mem2d, wk, wv)


def _attn_kernel(x_ref, k_ref, v_ref, wq_ref, wo_ref, g_ref, b_ref, o_ref):
    x = x_ref[0]
    q = _dot(x.astype(BF16), wq_ref[...]).astype(BF16)
    k = k_ref[0].astype(BF16)
    v = v_ref[0].astype(BF16)
    scale = MEM_HEAD_DIM ** -0.5
    outs = []
    for h in range(MEM_HEADS):
        sl = slice(h * MEM_HEAD_DIM, (h + 1) * MEM_HEAD_DIM)
        s = _dot_nt(q[:, sl], k[:, sl]) * scale
        s = s - jnp.max(s, axis=-1, keepdims=True)
        p = jnp.exp(s)
        p = p / jnp.sum(p, axis=-1, keepdims=True)
        outs.append(_dot(p.astype(BF16), v[:, sl]))
    o = jnp.concatenate(outs, axis=1).astype(BF16)
    att = _dot(o, wo_ref[...])
    o_ref[0] = _layer_norm(ALPHA * x + att, g_ref[...], b_ref[...])


def _attn(x3d, k, v, wq, wo, g, b, tq):
    bsz, l, _ = x3d.shape
    m = k.shape[1]
    full = lambda i, j: (0, 0)
    seq = lambda i, j: (i, j, 0)
    per_b = lambda i, j: (i, 0, 0)
    return pl.pallas_call(
        _attn_kernel,
        grid=(bsz, l // tq),
        in_specs=[pl.BlockSpec((1, tq, D_MODEL), seq),
                  pl.BlockSpec((1, m, D_MODEL), per_b), pl.BlockSpec((1, m, D_MODEL), per_b),
                  pl.BlockSpec(wq.shape, full), pl.BlockSpec(wo.shape, full),
                  pl.BlockSpec(g.shape, full), pl.BlockSpec(b.shape, full)],
        out_specs=pl.BlockSpec((1, tq, D_MODEL), seq),
        out_shape=jax.ShapeDtypeStruct((bsz, l, D_MODEL), F32),
        compiler_params=_cparams("parallel", "parallel"),
    )(x3d, k, v, wq, wo, g, b)


def _softmax_rows(s):
    s = s - jnp.max(s, axis=-1, keepdims=True)
    p = jnp.exp(s)
    return p / jnp.sum(p, axis=-1, keepdims=True)


def _attn_cache_kernel(x_ref, k_ref, v_ref, wq_ref, wo_ref, g_ref, b_ref, o_ref):
    nb, l, _ = x_ref.shape
    x = x_ref[...].reshape(nb * l, D_MODEL)
    q = _dot(x.astype(BF16), wq_ref[...]).astype(BF16)
    scale = MEM_HEAD_DIM ** -0.5
    outs = []
    for i in range(nb):
        heads = []
        for h in range(MEM_HEADS):
            k = k_ref[0, i, :, h, :].astype(BF16)
            v = v_ref[0, i, :, h, :].astype(BF16)
            qh = q[i * l:(i + 1) * l, h * MEM_HEAD_DIM:(h + 1) * MEM_HEAD_DIM]
            p = _softmax_rows(_dot_nt(qh, k) * scale)
            heads.append(_dot(p.astype(BF16), v))
        outs.append(jnp.concatenate(heads, axis=1))
    o = jnp.concatenate(outs, axis=0).astype(BF16)
    att = _dot(o, wo_ref[...])
    o_ref[...] = _layer_norm(ALPHA * x + att, g_ref[...], b_ref[...]).reshape(nb, l, D_MODEL)


def _attn_cache(x3d, k5, v5, wq, wo, g, b, nb):
    bsz, l, _ = x3d.shape
    full = lambda i: (0, 0)
    kv_blk = pl.BlockSpec((1, nb) + k5.shape[2:], lambda i: (0, i, 0, 0, 0))
    return pl.pallas_call(
        _attn_cache_kernel,
        grid=(bsz // nb,),
        in_specs=[pl.BlockSpec((nb, l, D_MODEL), lambda i: (i, 0, 0)), kv_blk, kv_blk,
                  pl.BlockSpec(wq.shape, full), pl.BlockSpec(wo.shape, full),
                  pl.BlockSpec(g.shape, full), pl.BlockSpec(b.shape, full)],
        out_specs=pl.BlockSpec((nb, l, D_MODEL), lambda i: (i, 0, 0)),
        out_shape=jax.ShapeDtypeStruct((bsz, l, D_MODEL), F32),
        compiler_params=_cparams("parallel"),
    )(x3d, k5, v5, wq, wo, g, b)


def _router_kernel(x_ref, wh_ref, wl_ref, br_ref, comb_ref, pos_ref, cnt_ref):
    x = x_ref[...]
    tm = x.shape[0]
    xh, xl = _split(x)
    wh = wh_ref[...]
    logits = _dot_nt(wh, xh) + _dot_nt(wh, xl) + _dot_nt(wl_ref[...], xh)
    s = _sigmoid(logits)
    sb = s + br_ref[...]
    per = N_EXPERTS // N_EXPERT_GROUPS
    neg = -jnp.inf

    sb3 = sb.reshape(N_EXPERT_GROUPS, per, tm)
    mem = lax.broadcasted_iota(jnp.int32, sb3.shape, 1)
    m1 = jnp.max(sb3, axis=1, keepdims=True)
    first = jnp.min(jnp.where(sb3 == m1, mem, per), axis=1, keepdims=True)
    m2 = jnp.max(jnp.where(mem == first, neg, sb3), axis=1, keepdims=True)
    gs = (m1 + m2).reshape(N_EXPERT_GROUPS, tm)

    def rank_rows(v):
        n = v.shape[0]
        idx = lax.broadcasted_iota(jnp.int32, v.shape, 0)
        rank = jnp.zeros(v.shape, jnp.int32)
        for j in range(n):
            rj = v[j:j + 1, :]
            beats = (rj > v) | ((rj == v) & (idx > j))
            rank = rank + beats.astype(jnp.int32)
        return rank

    gsel = rank_rows(gs) < TOPK_GROUPS
    emask = jnp.broadcast_to(gsel.reshape(N_EXPERT_GROUPS, 1, tm), sb3.shape).reshape(N_EXPERTS, tm)
    sm = jnp.where(emask, sb, neg)
    sel = (rank_rows(sm) < TOP_K) & emask
    gates = jnp.where(sel, s, 0.0)
    denom = jnp.sum(gates, axis=0, keepdims=True)
    comb_ref[...] = gates / denom * ROUTE_SCALE

    self = jnp.where(sel, 1.0, 0.0)
    before = (lax.broadcasted_iota(jnp.int32, (tm, tm), 0)
              < lax.broadcasted_iota(jnp.int32, (tm, tm), 1)).astype(BF16)
    rank_tok = _dot(self.astype(BF16), before)
    pos_ref[...] = jnp.where(sel, rank_tok.astype(jnp.int32), -1)
    cnt = jnp.sum(self, axis=1, keepdims=True).astype(jnp.int32)
    cnt_ref[0] = jnp.broadcast_to(cnt, cnt_ref.shape[1:])


def _router(x2d, wr_hi_t, wr_lo_t, br_col, tm):
    t = x2d.shape[0]
    nt = t // tm
    full = lambda i: (0, 0)
    return pl.pallas_call(
        _router_kernel,
        grid=(nt,),
        in_specs=[pl.BlockSpec((tm, D_MODEL), lambda i: (i, 0)), pl.BlockSpec(wr_hi_t.shape, full),
                  pl.BlockSpec(wr_lo_t.shape, full), pl.BlockSpec(br_col.shape, full)],
        out_specs=[pl.BlockSpec((N_EXPERTS, tm), lambda i: (0, i)),
                   pl.BlockSpec((N_EXPERTS, tm), lambda i: (0, i)),
                   pl.BlockSpec((1, N_EXPERTS, 128), lambda i: (i, 0, 0))],
        out_shape=[jax.ShapeDtypeStruct((N_EXPERTS, t), F32),
                   jax.ShapeDtypeStruct((N_EXPERTS, t), jnp.int32),
                   jax.ShapeDtypeStruct((nt, N_EXPERTS, 128), jnp.int32)],
        compiler_params=_cparams("parallel"),
    )(x2d, wr_hi_t, wr_lo_t, br_col)


def _moe_rows(t, nt):
    worst = TOP_K * t + nt * N_EXPERTS * (MOE_SEG - 1) + N_EXPERTS * (MOE_TR - 1)
    return -(-worst // MOE_TR) * MOE_TR


def _moe_local_rows(tm):
    worst = TOP_K * tm + N_EXPERTS * (MOE_SEG - 1)
    return -(-worst // 512) * 512


def _moe_plan(cnt, tm):
    nt = cnt.shape[0]
    npad = (cnt + (MOE_SEG - 1)) // MOE_SEG * MOE_SEG
    loc_off = jnp.cumsum(npad, axis=1) - npad
    tot = jnp.sum(npad, axis=0)
    totp = (tot + (MOE_TR - 1)) // MOE_TR * MOE_TR
    gend = jnp.cumsum(totp)
    goff = gend - totp
    seg_off = goff[None, :] + jnp.cumsum(npad, axis=0) - npad
    n_row_tiles = _moe_rows(nt * tm, nt) // MOE_TR
    tile_start = jnp.arange(n_row_tiles, dtype=jnp.int32) * MOE_TR
    ended = (gend[None, :] <= tile_start[:, None]).astype(jnp.int32)
    tile_expert = jnp.minimum(jnp.sum(ended, axis=1), N_EXPERTS - 1)
    n_valid = (gend[-1] // MOE_TR).reshape(1)
    i32 = lambda a: a.astype(jnp.int32)
    return dict(npad=i32(npad.reshape(-1)), loc_off=i32(loc_off.reshape(-1)), seg_off=i32(seg_off.reshape(-1)),
                fill_off=i32(goff + tot), fill_n=i32((totp - tot) // MOE_SEG),
                tile_expert=i32(tile_expert), n_valid=i32(n_valid))


def _for_each_segment(i, npad_ref, loc_ref, seg_ref, body):
    def per_expert(e, n):
        k = i * N_EXPERTS + e
        nch = npad_ref[k] // MOE_SEG
        base = loc_ref[k]
        dst = seg_ref[k]

        def chunk(j, c):
            body(e, j, pl.multiple_of(base + j * MOE_SEG, MOE_SEG), pl.multiple_of(dst + j * MOE_SEG, MOE_SEG))
            return c

        lax.fori_loop(0, nch, chunk, 0)
        return n + nch

    return lax.fori_loop(0, N_EXPERTS, per_expert, 0)


def _one_hot_rows(pos_ref, e, j, tm, value):
    row = pos_ref[pl.ds(e, 1), :]
    want = lax.broadcasted_iota(jnp.int32, (MOE_SEG, tm), 0) + j * MOE_SEG
    return jnp.where(row == want, value, 0.0).astype(BF16)


def _moe_gather_kernel(npad_ref, loc_ref, seg_ref, foff_ref, fn_ref, x_ref, pos_ref, xs_hbm,
                       p_scr, xs_scr, zero_scr, sem):
    i = pl.program_id(0)
    tm = x_ref.shape[0]
    p_scr[...] = jnp.zeros_like(p_scr)

    def put(e, j, lrow, grow):
        p_scr[pl.ds(lrow, MOE_SEG), :] = _one_hot_rows(pos_ref, e, j, tm, 1.0)

    _for_each_segment(i, npad_ref, loc_ref, seg_ref, put)
    xs_scr[...] = _dot(p_scr[...], x_ref[...].astype(BF16)).astype(BF16)

    def send(e, j, lrow, grow):
        pltpu.make_async_copy(xs_scr.at[pl.ds(lrow, MOE_SEG)], xs_hbm.at[pl.ds(grow, MOE_SEG)], sem).start()

    n_sent = _for_each_segment(i, npad_ref, loc_ref, seg_ref, send)

    def wait_one(k, c):
        pltpu.make_async_copy(xs_scr.at[pl.ds(0, MOE_SEG)], xs_hbm.at[pl.ds(0, MOE_SEG)], sem).wait()
        return c

    lax.fori_loop(0, n_sent, wait_one, 0)

    @pl.when(i == pl.num_programs(0) - 1)
    def _():
        zero_scr[...] = jnp.zeros_like(zero_scr)

        def per_expert(e, n):
            def chunk(j, c):
                grow = pl.multiple_of(foff_ref[e] + j * MOE_SEG, MOE_SEG)
                pltpu.make_async_copy(zero_scr, xs_hbm.at[pl.ds(grow, MOE_SEG)], sem).start()
                return c
            lax.fori_loop(0, fn_ref[e], chunk, 0)
            return n + fn_ref[e]

        n_fill = lax.fori_loop(0, N_EXPERTS, per_expert, 0)
        lax.fori_loop(0, n_fill, wait_one, 0)


def _moe_gather(plan, x2d, pos, tm):
    t = x2d.shape[0]
    nt = t // tm
    rows = _moe_rows(t, nt)
    rl = _moe_local_rows(tm)
    return pl.pallas_call(
        _moe_gather_kernel,
        grid_spec=pltpu.PrefetchScalarGridSpec(
            num_scalar_prefetch=5,
            grid=(nt,),
            in_specs=[pl.BlockSpec((tm, D_MODEL), lambda i, *_: (i, 0)),
                      pl.BlockSpec((N_EXPERTS, tm), lambda i, *_: (0, i))],
            out_specs=pl.BlockSpec(memory_space=pl.ANY),
            scratch_shapes=[pltpu.VMEM((rl, tm), BF16), pltpu.VMEM((rl, D_MODEL), BF16),
                            pltpu.VMEM((MOE_SEG, D_MODEL), BF16), pltpu.SemaphoreType.DMA]),
        out_shape=jax.ShapeDtypeStruct((rows, D_MODEL), BF16),
        compiler_params=_cparams("arbitrary"),
    )(plan['npad'], plan['loc_off'], plan['seg_off'], plan['fill_off'], plan['fill_n'], x2d, pos)


def _moe_expert_kernel(te_ref, nv_ref, xs_ref, wg_ref, wu_ref, wd_ref, ys_ref, wg_s, wu_s, wd_s):
    j = pl.program_id(0)
    changed = te_ref[j] != te_ref[jnp.maximum(j - 1, 0)]

    @pl.when((j == 0) | changed)
    def _():
        wg_s[...] = wg_ref[0].astype(BF16)
        wu_s[...] = wu_ref[0].astype(BF16)
        wd_s[...] = wd_ref[0].astype(BF16)

    @pl.when(j < nv_ref[0])
    def _():
        x = xs_ref[...]
        hid = _silu(_dot(x, wg_s[...])) * _dot(x, wu_s[...])
        ys_ref[...] = _dot(hid.astype(BF16), wd_s[...]).astype(ys_ref.dtype)


def _moe_experts(plan, xs, wg, wu, wd):
    rows = xs.shape[0]
    de = wg.shape[2]
    row_tile = lambda j, te, nv: (jnp.minimum(j, nv[0] - 1), 0)
    expert = lambda j, te, nv: (te[jnp.minimum(j, nv[0] - 1)], 0, 0)
    return pl.pallas_call(
        _moe_expert_kernel,
        grid_spec=pltpu.PrefetchScalarGridSpec(
            num_scalar_prefetch=2,
            grid=(rows // MOE_TR,),
            in_specs=[pl.BlockSpec((MOE_TR, D_MODEL), row_tile),
                      pl.BlockSpec((1, D_MODEL, de), expert), pl.BlockSpec((1, D_MODEL, de), expert),
                      pl.BlockSpec((1, de, D_MODEL), expert)],
            out_specs=pl.BlockSpec((MOE_TR, D_MODEL), row_tile),
            scratch_shapes=[pltpu.VMEM((D_MODEL, de), BF16), pltpu.VMEM((D_MODEL, de), BF16),
                            pltpu.VMEM((de, D_MODEL), BF16)]),
        out_shape=jax.ShapeDtypeStruct((rows, D_MODEL), BF16),
        compiler_params=_cparams("arbitrary"),
    )(plan['tile_expert'], plan['n_valid'], xs, wg, wu, wd)


def _moe_combine_kernel(npad_ref, loc_ref, seg_ref, x_ref, pos_ref, comb_ref, ys_hbm, sg_ref, su_ref,
                        sd_ref, g_ref, b_ref, o_ref, pg_scr, ys_scr, sem):
    i = pl.program_id(0)
    tm = x_ref.shape[0]

    @pl.when(i == 0)
    def _():
        ys_scr[...] = jnp.zeros_like(ys_scr)

    def fetch(e, j, lrow, grow):
        pltpu.make_async_copy(ys_hbm.at[pl.ds(grow, MOE_SEG)], ys_scr.at[pl.ds(lrow, MOE_SEG)], sem).start()

    n_fetch = _for_each_segment(i, npad_ref, loc_ref, seg_ref, fetch)

    pg_scr[...] = jnp.zeros_like(pg_scr)

    def put(e, j, lrow, grow):
        pg_scr[pl.ds(lrow, MOE_SEG), :] = _one_hot_rows(pos_ref, e, j, tm, comb_ref[pl.ds(e, 1), :])

    _for_each_segment(i, npad_ref, loc_ref, seg_ref, put)

    x = x_ref[...]
    xb = x.astype(BF16)
    hs = _silu(_dot(xb, sg_ref[...])) * _dot(xb, su_ref[...])
    shared = _dot(hs.astype(BF16), sd_ref[...])

    def wait_one(k, c):
        pltpu.make_async_copy(ys_hbm.at[pl.ds(0, MOE_SEG)], ys_scr.at[pl.ds(0, MOE_SEG)], sem).wait()
        return c

    lax.fori_loop(0, n_fetch, wait_one, 0)
    routed = _dot_tn(pg_scr[...], ys_scr[...])
    o_ref[...] = _layer_norm(ALPHA * x + (routed + shared), g_ref[...], b_ref[...])


def _moe_combine(plan, x2d, pos, comb, ys, sg, su, sd, g, b, tm):
    t = x2d.shape[0]
    nt = t // tm
    rl = _moe_local_rows(tm)
    full = lambda i, *_: (0, 0)
    return pl.pallas_call(
        _moe_combine_kernel,
        grid_spec=pltpu.PrefetchScalarGridSpec(
            num_scalar_prefetch=3,
            grid=(nt,),
            in_specs=[pl.BlockSpec((tm, D_MODEL), lambda i, *_: (i, 0)),
                      pl.BlockSpec((N_EXPERTS, tm), lambda i, *_: (0, i)),
                      pl.BlockSpec((N_EXPERTS, tm), lambda i, *_: (0, i)),
                      pl.BlockSpec(memory_space=pl.ANY),
                      pl.BlockSpec(sg.shape, full), pl.BlockSpec(su.shape, full), pl.BlockSpec(sd.shape, full),
                      pl.BlockSpec(g.shape, full), pl.BlockSpec(b.shape, full)],
            out_specs=pl.BlockSpec((tm, D_MODEL), lambda i, *_: (i, 0)),
            scratch_shapes=[pltpu.VMEM((rl, tm), BF16), pltpu.VMEM((rl, D_MODEL), BF16),
                            pltpu.SemaphoreType.DMA]),
        out_shape=jax.ShapeDtypeStruct((t, D_MODEL), F32),
        compiler_params=_cparams("arbitrary"),
    )(plan['npad'], plan['loc_off'], plan['seg_off'], x2d, pos, comb, ys, sg, su, sd, g, b)


def _moe(x2d, w, tm):
    comb, pos, cnt = _router(x2d, w['wr_hi_t'], w['wr_lo_t'], w['br_col'], tm)
    plan = _moe_plan(cnt[:, :, 0], tm)
    xs = _moe_gather(plan, x2d, pos, tm)
    ys = _moe_experts(plan, xs, w['w_e_gate'], w['w_e_up'], w['w_e_down'])
    return _moe_combine(plan, x2d, pos, comb, ys, w['w_s_gate'], w['w_s_up'], w['w_s_down'],
                        w['ln3_g'], w['ln3_b'], tm)


def _layer(x3d, mem_k, mem_v, conv_st, ssm_st, s5_re, s5_im, w, *, l_valid, s5_chunk, tm, tq):
    bsz, lp, _ = x3d.shape
    t = bsz * lp
    x2d = x3d.reshape(t, D_MODEL)
    tm = min(tm, t)
    tq = min(tq, lp)
    z, xbc, dt = _in_proj(x2d, w['wz'], w['wx'], w['wd'], tm)
    q_in = min(lp, SSD_CHUNK)
    h0_t = jnp.transpose(ssm_st.reshape(bsz, D_SSM, SSM_STATE), (0, 2, 1))
    ym, conv_new, h_t = _ssd(xbc.reshape(bsz, lp, D_CONV), z.reshape(bsz, lp, D_SSM),
                             dt.reshape(bsz, lp, SSM_HEADS), conv_st, h0_t,
                             w['conv_w'], w['conv_b'], w['dt_bias'], w['a_log'], w['d_skip_x'],
                             w['norm_ssm'], q_in, min(l_valid, q_in))
    ssm_new = jnp.transpose(h_t, (0, 2, 1)).reshape(bsz, SSM_HEADS, SSM_HEAD_DIM, SSM_STATE)
    n_chunks = l_valid // s5_chunk
    nbc = bsz * n_chunks
    slots = lp // n_chunks
    tn = min(SLOT_TN, nbc)
    x3 = x3d.reshape(nbc, slots, D_MODEL)
    ut = _in_proj_u(x3, w['wu_t'], s5_chunk, tn)
    yt, s5_re_new, s5_im_new = _s5(ut, s5_re, s5_im, w['s5_tables'][s5_chunk], bsz)
    x1 = _mix(ym.reshape(nbc, slots, D_SSM), yt, x3, w['w_glu'], w['b_glu'], w['w_mix_out'],
              w['ln1_g'], w['ln1_b'], tn)
    attn_args = (x1.reshape(bsz, lp, D_MODEL), mem_k, mem_v, w['w_q'], w['w_o'], w['ln2_g'], w['ln2_b'])
    if mem_k.ndim == 5:
        x2 = _attn_cache(*attn_args, min(ATTN_CACHE_NB, bsz))
    else:
        x2 = _attn(*attn_args, tq)
    return x2.reshape(t, D_MODEL), conv_new, ssm_new, s5_re_new, s5_im_new


def kernel(x_prompt, x_sample, mem_prompt, state_conv, state_ssm, state_s5_re, state_s5_im, cache_mem_k, cache_mem_v, w_in, conv_w, conv_b, dt_bias, a_log, d_ssm, norm_ssm, s5_a_re, s5_a_im, s5_log_step, s5_b_re, s5_b_im, s5_c_re, s5_c_im, s5_d, w_glu, b_glu, w_mix_out, ln1_g, ln1_b, w_q, w_k, w_v, w_o, ln2_g, ln2_b, w_router, b_router, w_e_gate, w_e_up, w_e_down, w_s_gate, w_s_up, w_s_down, ln3_g, ln3_b):
    assert w_in.shape[0] == DEPTH
    bp, lp, _ = x_prompt.shape
    bs, ls, _ = x_sample.shape
    row = lambda a: a.reshape(1, -1)
    bf = lambda a: a.astype(BF16)
    w_in0 = w_in[0]
    o1 = D_SSM
    o2 = o1 + D_CONV
    o3 = o2 + SSM_HEADS
    wr_hi, wr_lo = _split(w_router[0].T)
    s5_args = (s5_a_re[0], s5_a_im[0], s5_log_step[0], s5_b_re[0], s5_b_im[0], s5_c_re[0], s5_c_im[0], s5_d[0])
    s5_chunk_p = math.gcd(lp, 16)
    w = {
        'wz': bf(w_in0[:, :o1]), 'wx': bf(w_in0[:, o1:o2]), 'wd': bf(w_in0[:, o2:o3]), 'wu_t': bf(w_in0[:, o3:].T),
        'conv_w': conv_w[0], 'conv_b': row(conv_b[0]), 'dt_bias': row(dt_bias[0]), 'a_log': row(a_log[0]),
        'd_skip_x': row(jnp.repeat(d_ssm[0], SSM_HEAD_DIM)), 'norm_ssm': row(norm_ssm[0]),
        's5_tables': {q: _s5_tables(*s5_args, q) for q in sorted({s5_chunk_p, ls})},
        'w_glu': bf(w_glu[0]), 'b_glu': row(b_glu[0]), 'w_mix_out': bf(w_mix_out[0]),
        'ln1_g': row(ln1_g[0]), 'ln1_b': row(ln1_b[0]),
        'w_q': bf(w_q[0]), 'w_o': bf(w_o[0]), 'ln2_g': row(ln2_g[0]), 'ln2_b': row(ln2_b[0]),
        'wr_hi_t': wr_hi, 'wr_lo_t': wr_lo, 'br_col': b_router[0].reshape(-1, 1),
        'w_e_gate': w_e_gate[0], 'w_e_up': w_e_up[0], 'w_e_down': w_e_down[0],
        'w_s_gate': bf(w_s_gate[0]), 'w_s_up': bf(w_s_up[0]), 'w_s_down': bf(w_s_down[0]),
        'ln3_g': row(ln3_g[0]), 'ln3_b': row(ln3_b[0]),
    }

    m_tok = mem_prompt.shape[1]
    mk2d, mv2d = _memkv(mem_prompt.reshape(bp * m_tok, D_MODEL), bf(w_k[0]), bf(w_v[0]), 512)
    mk = mk2d.reshape(bp, m_tok, D_MODEL)
    mv = mv2d.reshape(bp, m_tok, D_MODEL)
    zeros = lambda *s: jnp.zeros(s, F32)
    x2p, conv_p, ssm_p, s5r_p, s5i_p = _layer(
        x_prompt, mk, mv, zeros(bp, CONV_W - 1, D_CONV), zeros(bp, SSM_HEADS, SSM_HEAD_DIM, SSM_STATE),
        zeros(bp, S5_GROUPS, S5_STATE), zeros(bp, S5_GROUPS, S5_STATE), w,
        l_valid=lp, s5_chunk=s5_chunk_p, tm=512, tq=512)

    xs_pad = jnp.pad(x_sample, ((0, 0), (0, SAMPLE_PAD - ls), (0, 0)))
    x2s, conv_s, ssm_s, s5r_s, s5i_s = _layer(
        xs_pad, cache_mem_k, cache_mem_v, state_conv[0], state_ssm[0], state_s5_re[0], state_s5_im[0], w,
        l_valid=ls, s5_chunk=ls, tm=512, tq=SAMPLE_PAD)

    tp = bp * lp
    y_all = _moe(jnp.concatenate([x2p, x2s], axis=0), w, MOE_TM)
    yp = y_all[:tp].reshape(bp, lp, D_MODEL)
    ys = y_all[tp:].reshape(bs, SAMPLE_PAD, D_MODEL)[:, :ls]

    kv_shape = (1, bp, m_tok, MEM_HEADS, MEM_HEAD_DIM)
    return (yp, ys, conv_p[None], ssm_p[None], s5r_p[None], s5i_p[None],
            mk.reshape(kv_shape), mv.reshape(kv_shape),
            conv_s[None], ssm_s[None], s5r_s[None], s5i_s[None])
```

```python
import functools
import math

import jax
import jax.numpy as jnp
from jax import lax
from jax.experimental import pallas as pl
from jax.experimental.pallas import tpu as pltpu

F32 = jnp.float32
BF16 = jnp.bfloat16

D_MODEL = 1024
D_SSM = 1024
SSM_HEAD_DIM = 64
SSM_HEADS = D_SSM // SSM_HEAD_DIM
SSM_GROUPS = 2
SSM_STATE = 128
CONV_W = 4
D_CONV = D_SSM + 2 * SSM_GROUPS * SSM_STATE
D_S5 = 1024
S5_GROUP_CH = 16
S5_GROUPS = D_S5 // S5_GROUP_CH
S5_STATE = 64
MEM_HEADS = 4
MEM_HEAD_DIM = D_MODEL // MEM_HEADS
N_EXPERTS = 64
N_EXPERT_GROUPS = 8
TOPK_GROUPS = 4
TOP_K = 6
ROUTE_SCALE = 2.5
DEPTH = 1
ALPHA = (2.0 * DEPTH) ** 0.25
LN_EPS = 1e-5
RMS_EPS = 1e-5

SSD_CHUNK = 128
SAMPLE_PAD = 8
SLOT_BLK = 8
SLOT_TN = 128
ATTN_CACHE_NB = 4
MOE_TM = 512
MOE_TR = 512
MOE_SEG = 16
VMEM_LIMIT = 48 * 1024 * 1024


def _cparams(*sem):
    return pltpu.CompilerParams(dimension_semantics=sem, vmem_limit_bytes=VMEM_LIMIT)


def _dot(a, b):
    return jnp.dot(a, b, preferred_element_type=F32)


def _dot_nt(a, b):
    return lax.dot_general(a, b, (((1,), (1,)), ((), ())), preferred_element_type=F32)


def _dot_tn(a, b):
    return lax.dot_general(a, b, (((0,), (0,)), ((), ())), preferred_element_type=F32)


def _split(v):
    hi = v.astype(BF16)
    lo = (v - hi.astype(F32)).astype(BF16)
    return hi, lo


def _silu(x):
    return x * (1.0 / (1.0 + jnp.exp(-x)))


def _sigmoid(x):
    return 1.0 / (1.0 + jnp.exp(-x))


def _softplus(x):
    return jnp.maximum(x, 0.0) + jnp.log(1.0 + jnp.exp(-jnp.abs(x)))


def _layer_norm(x, g, b):
    mu = jnp.mean(x, axis=-1, keepdims=True)
    xc = x - mu
    var = jnp.mean(xc * xc, axis=-1, keepdims=True)
    return xc * lax.rsqrt(var + LN_EPS) * g + b


def _in_proj_kernel(x_ref, wz_ref, wx_ref, wd_ref, z_ref, xbc_ref, dt_ref):
    x = x_ref[...].astype(BF16)
    z_ref[...] = _dot(x, wz_ref[...]).astype(z_ref.dtype)
    xbc_ref[...] = _dot(x, wx_ref[...])
    dt_ref[...] = _dot(x, wd_ref[...])


def _in_proj(x2d, wz, wx, wd, tm):
    t = x2d.shape[0]
    full = lambda i: (0, 0)
    row = lambda i: (i, 0)
    return pl.pallas_call(
        _in_proj_kernel,
        grid=(t // tm,),
        in_specs=[pl.BlockSpec((tm, D_MODEL), row),
                  pl.BlockSpec(wz.shape, full), pl.BlockSpec(wx.shape, full), pl.BlockSpec(wd.shape, full)],
        out_specs=[pl.BlockSpec((tm, D_SSM), row), pl.BlockSpec((tm, D_CONV), row),
                   pl.BlockSpec((tm, SSM_HEADS), row)],
        out_shape=[jax.ShapeDtypeStruct((t, D_SSM), BF16), jax.ShapeDtypeStruct((t, D_CONV), F32),
                   jax.ShapeDtypeStruct((t, SSM_HEADS), F32)],
        compiler_params=_cparams("parallel"),
    )(x2d, wz, wx, wd)


def _in_proj_u_kernel(x_ref, wu_ref, u_ref):
    np_, _, qb, h, tn = u_ref.shape
    for t in range(qb):
        ut = _dot_nt(wu_ref[...], x_ref[:, t, :].astype(BF16))
        u_ref[:, :, t] = ut.astype(u_ref.dtype).reshape(np_, 2, h, tn)


def _in_proj_u(x3, wu_t, qs, tn):
    nbc = x3.shape[0]
    np_ = S5_GROUPS // 2
    qb = min(qs, SLOT_BLK)
    return pl.pallas_call(
        _in_proj_u_kernel,
        grid=(qs // qb, nbc // tn),
        in_specs=[pl.BlockSpec((tn, SLOT_BLK, D_MODEL), lambda t, j: (j, t, 0)),
                  pl.BlockSpec(wu_t.shape, lambda t, j: (0, 0))],
        out_specs=pl.BlockSpec((np_, 2, qb, S5_GROUP_CH, tn), lambda t, j: (0, 0, t, 0, j)),
        out_shape=jax.ShapeDtypeStruct((np_, 2, qs, S5_GROUP_CH, nbc), BF16),
        compiler_params=_cparams("parallel", "parallel"),
    )(x3, wu_t)


def _ssd_kernel(xbc_ref, z_ref, dt_ref, cst_ref, h0_ref, cw_ref, cb_ref, dtb_ref, alog_ref,
                dskip_ref, nw_ref, y_ref, cnew_ref, hout_ref, h_scr, xp_scr, dt_scr,
                *, q, q_in, l_valid):
    c = pl.program_id(1)
    halo = CONV_W - 1
    base = 8

    @pl.when(c == 0)
    def _():
        h_scr[...] = h0_ref[0]
        xp_scr[...] = jnp.zeros_like(xp_scr)
        dt_scr[...] = jnp.zeros_like(dt_scr)
        xp_scr[base - halo:base, :] = cst_ref[0]

    xp_scr[base:base + q_in, :] = xbc_ref[0]
    dt_scr[0:q_in, :] = dt_ref[0]

    w = cw_ref[...]
    acc = cb_ref[...] + w[0:1] * xp_scr[pl.ds(base - halo, q), :]
    for k in range(1, CONV_W):
        acc = acc + w[k:k + 1] * xp_scr[pl.ds(base - halo + k, q), :]
    tail = xp_scr[base + l_valid - halo:base + l_valid, :]
    cnew_ref[0] = tail
    xp_scr[base - halo:base, :] = tail

    xa = _silu(acc)
    xs = xa[:, :D_SSM]
    nbc = SSM_GROUPS * SSM_STATE
    bm = xa[:, D_SSM:D_SSM + nbc].astype(BF16)
    cm = xa[:, D_SSM + nbc:].astype(BF16)

    row = lax.broadcasted_iota(jnp.int32, (q, q), 0)
    col = lax.broadcasted_iota(jnp.int32, (q, q), 1)
    causal = row >= col
    tri = causal.astype(BF16)

    dt = _softplus(dt_scr[...] + dtb_ref[...])
    if l_valid < q:
        dt = jnp.where(lax.broadcasted_iota(jnp.int32, dt.shape, 0) < l_valid, dt, 0.0)
    d_a = dt * (-jnp.exp(alog_ref[...]))
    a_hi, a_lo = _split(d_a)
    acum = _dot(tri, a_hi) + _dot(tri, a_lo)
    eye = (lax.broadcasted_iota(jnp.int32, (SSM_HEADS, SSM_HEADS), 0)
           == lax.broadcasted_iota(jnp.int32, (SSM_HEADS, SSM_HEADS), 1)).astype(BF16)
    c_hi, c_lo = _split(acum)
    acum_t = _dot_nt(eye, c_hi) + _dot_nt(eye, c_lo)

    lane_head = lax.broadcasted_iota(jnp.int32, (SSM_HEADS, D_SSM), 1) // SSM_HEAD_DIM
    expand = (lane_head == lax.broadcasted_iota(jnp.int32, (SSM_HEADS, D_SSM), 0)).astype(BF16)

    def ex(v):
        hi, lo = _split(v)
        return _dot(hi, expand) + _dot(lo, expand)

    acum_last = acum[q - 1:q, :]
    xdt = xs * ex(dt)
    xdt_b = xdt.astype(BF16)
    xw_b = (xdt * ex(jnp.exp(acum_last - acum))).astype(BF16)
    dec_in = ex(jnp.exp(acum))
    dec_chunk = ex(jnp.broadcast_to(jnp.exp(acum_last), (8, SSM_HEADS)))[0:1]

    heads_per_group = SSM_HEADS // SSM_GROUPS
    gw = D_SSM // SSM_GROUPS
    h_old = h_scr[...]
    h_old_b = h_old.astype(BF16)
    lane2 = lax.broadcasted_iota(jnp.int32, (q, 2 * SSM_HEAD_DIM), 1)
    y_parts = []
    new_parts = []
    for g in range(SSM_GROUPS):
        bg = bm[:, g * SSM_STATE:(g + 1) * SSM_STATE]
        cg = cm[:, g * SSM_STATE:(g + 1) * SSM_STATE]
        cb = _dot_nt(cg, bg)
        y_off = _dot(cg, h_old_b[:, g * gw:(g + 1) * gw])
        new_parts.append(_dot_tn(bg, xw_b[:, g * gw:(g + 1) * gw]))
        pair_out = []
        for pr in range(heads_per_group // 2):
            lo_l = g * gw + pr * 2 * SSM_HEAD_DIM
            xp2 = xdt_b[:, lo_l:lo_l + 2 * SSM_HEAD_DIM]
            accp = None
            for sub in range(2):
                h = g * heads_per_group + pr * 2 + sub
                seg = acum[:, h:h + 1] - acum_t[h:h + 1, :]
                lmat = jnp.exp(jnp.where(causal, seg, -jnp.inf))
                gmat = (cb * lmat).astype(BF16)
                in_head = (lane2 // SSM_HEAD_DIM) == sub
                rhs = jnp.where(in_head, xp2, jnp.zeros_like(xp2))
                part = _dot(gmat, rhs)
                accp = part if accp is None else accp + part
            pair_out.append(accp)
        y_diag = jnp.concatenate(pair_out, axis=1)
        y_parts.append(y_diag + y_off * dec_in[:, g * gw:(g + 1) * gw])
    y = jnp.concatenate(y_parts, axis=1) + dskip_ref[...] * xs
    h_new = h_old * dec_chunk + jnp.concatenate(new_parts, axis=1)
    h_scr[...] = h_new
    hout_ref[0] = h_new

    gy = y[0:q_in] * _silu(z_ref[0].astype(F32))
    outs = []
    for g in range(SSM_GROUPS):
        sg = gy[:, g * gw:(g + 1) * gw]
        ms = jnp.mean(sg * sg, axis=-1, keepdims=True)
        outs.append(sg * lax.rsqrt(ms + RMS_EPS))
    y_ref[0] = (jnp.concatenate(outs, axis=1) * nw_ref[...]).astype(y_ref.dtype)


def _ssd(xbc, z, dt, conv_st, h0_t, conv_w, conv_b, dt_bias, a_log, d_skip_x, norm_w, q_in, l_valid):
    b, l, _ = xbc.shape
    q = SSD_CHUNK
    nc = l // q_in
    kern = functools.partial(_ssd_kernel, q=q, q_in=q_in, l_valid=l_valid)
    seq = lambda i, c: (i, c, 0)
    per_b = lambda i, c: (i, 0, 0)
    full = lambda i, c: (0, 0)
    return pl.pallas_call(
        kern,
        grid=(b, nc),
        in_specs=[pl.BlockSpec((1, q_in, D_CONV), seq), pl.BlockSpec((1, q_in, D_SSM), seq),
                  pl.BlockSpec((1, q_in, SSM_HEADS), seq),
                  pl.BlockSpec((1, CONV_W - 1, D_CONV), per_b),
                  pl.BlockSpec((1, SSM_STATE, D_SSM), per_b),
                  pl.BlockSpec(conv_w.shape, full), pl.BlockSpec(conv_b.shape, full),
                  pl.BlockSpec(dt_bias.shape, full), pl.BlockSpec(a_log.shape, full),
                  pl.BlockSpec(d_skip_x.shape, full), pl.BlockSpec(norm_w.shape, full)],
        out_specs=[pl.BlockSpec((1, q_in, D_SSM), seq),
                   pl.BlockSpec((1, CONV_W - 1, D_CONV), per_b),
                   pl.BlockSpec((1, SSM_STATE, D_SSM), per_b)],
        out_shape=[jax.ShapeDtypeStruct((b, l, D_SSM), F32),
                   jax.ShapeDtypeStruct((b, CONV_W - 1, D_CONV), F32),
                   jax.ShapeDtypeStruct((b, SSM_STATE, D_SSM), F32)],
        scratch_shapes=[pltpu.VMEM((SSM_STATE, D_SSM), F32),
                        pltpu.VMEM((q + 8, D_CONV), F32),
                        pltpu.VMEM((q, SSM_HEADS), F32)],
        compiler_params=_cparams("parallel", "arbitrary"),
    )(xbc, z, dt, conv_st, h0_t, conv_w, conv_b, dt_bias, a_log, d_skip_x, norm_w)


def _s5_tables(a_re, a_im, log_step, b_re, b_im, c_re, c_im, d, qs):
    g, p = a_re.shape
    h = S5_GROUP_CH
    r = qs * h
    step = jnp.exp(log_step)[:, None]
    k = jnp.arange(qs + 1, dtype=F32)[:, None, None]
    mag = jnp.exp(a_re * step * k)
    ang = a_im * step * k
    lr = mag * jnp.cos(ang)
    li = mag * jnp.sin(ang)
    ab_re, ab_im = lr[1], li[1]
    den = a_re * a_re + a_im * a_im
    num_re = ab_re - 1.0
    cf_re = (num_re * a_re + ab_im * a_im) / den
    cf_im = (ab_im * a_re - num_re * a_im) / den
    bb_re = cf_re[..., None] * b_re - cf_im[..., None] * b_im
    bb_im = cf_re[..., None] * b_im + cf_im[..., None] * b_re
    hp = lax.Precision.HIGHEST
    lb_re = lr[:qs, :, :, None] * bb_re - li[:qs, :, :, None] * bb_im
    lb_im = lr[:qs, :, :, None] * bb_im + li[:qs, :, :, None] * bb_re
    kern = (jnp.einsum('ghp,kgpj->kghj', c_re, lb_re, precision=hp)
            - jnp.einsum('ghp,kgpj->kghj', c_im, lb_im, precision=hp))
    none = jnp.zeros_like(kern[0])
    m = jnp.stack([jnp.stack([kern[tp - t] if t <= tp else none for t in range(qs)], axis=2)
                   for tp in range(qs)], axis=1).reshape(g, r, r)
    rev_re = lr[:qs][::-1]
    rev_im = li[:qs][::-1]
    bp_re = rev_re[..., None] * bb_re - rev_im[..., None] * bb_im
    bp_im = rev_re[..., None] * bb_im + rev_im[..., None] * bb_re
    bp_re = jnp.transpose(bp_re, (1, 2, 0, 3)).reshape(g, p, r)
    bp_im = jnp.transpose(bp_im, (1, 2, 0, 3)).reshape(g, p, r)
    fr = lr[1:, :, None, :]
    fi = li[1:, :, None, :]
    cy_re = c_re[None] * fr - c_im[None] * fi
    cy_im = -(c_re[None] * fi + c_im[None] * fr)
    cy_re = jnp.transpose(cy_re, (1, 0, 2, 3)).reshape(g, r, p)
    cy_im = jnp.transpose(cy_im, (1, 0, 2, 3)).reshape(g, r, p)

    np_ = g // 2
    z_pr = jnp.zeros((np_, p, r), F32)
    e = lambda a: a.reshape(np_, 2, *a.shape[1:])
    bre, bim = e(bp_re), e(bp_im)
    bp = jnp.concatenate([
        jnp.concatenate([bre[:, 0], z_pr], axis=2), jnp.concatenate([z_pr, bre[:, 1]], axis=2),
        jnp.concatenate([bim[:, 0], z_pr], axis=2), jnp.concatenate([z_pr, bim[:, 1]], axis=2)], axis=1)
    z_rp = jnp.zeros((np_, r, p), F32)
    cre, cim = e(cy_re), e(cy_im)
    cp = jnp.concatenate([
        jnp.concatenate([cre[:, 0], z_rp, cim[:, 0], z_rp], axis=2),
        jnp.concatenate([z_rp, cre[:, 1], z_rp, cim[:, 1]], axis=2)], axis=1)
    lam_q = jnp.stack([lr[qs].reshape(np_, 2 * p), li[qs].reshape(np_, 2 * p)], axis=1)
    dcol = jnp.broadcast_to(d.reshape(np_, 2, 1, h), (np_, 2, qs, h)).reshape(np_, 2 * r, 1)
    bp_t = jnp.transpose(bp, (0, 2, 1))
    return e(m).astype(BF16), bp_t.astype(BF16), cp.astype(BF16), lam_q, dcol


def _gelu_tanh(x):
    return 0.5 * x * (1.0 + jnp.tanh(math.sqrt(2.0 / math.pi) * (x + 0.044715 * (x * x * x))))


def _s5_kernel(u_ref, m_ref, bp_ref, cp_ref, lam_ref, d_ref, s0_ref, y_ref, sout_ref,
               loc_re, loc_im, st_re, st_im, *, nb, nc):
    r = m_ref.shape[2]
    pw = lam_ref.shape[2]
    u = u_ref[0].reshape(2 * r, u_ref.shape[-1])
    loc_re[...] = _dot_tn(u, bp_ref[0, :, :pw])
    loc_im[...] = _dot_tn(u, bp_ref[0, :, pw:])
    ar = lam_ref[0, 0:1, :]
    ai = lam_ref[0, 1:2, :]
    sr = s0_ref[0, 0]
    si = s0_ref[0, 1]
    for c in range(nc):
        rows = pl.ds(c, nb, stride=nc) if nc > 1 else pl.ds(0, nb)
        st_re[rows, :] = sr
        st_im[rows, :] = si
        sr, si = (ar * sr - ai * si + loc_re[rows, :], ar * si + ai * sr + loc_im[rows, :])
    sout_ref[0, 0] = sr
    sout_ref[0, 1] = si
    y_off = (_dot_nt(cp_ref[0, :, :pw], st_re[...].astype(BF16))
             + _dot_nt(cp_ref[0, :, pw:], st_im[...].astype(BF16)))
    y_loc = jnp.concatenate([_dot(m_ref[0, 0], u[:r]), _dot(m_ref[0, 1], u[r:])], axis=0)
    y = y_loc + y_off + d_ref[0] * u.astype(F32)
    y_ref[0] = _gelu_tanh(y).astype(y_ref.dtype).reshape(y_ref.shape[1:])


def _s5(ut, s0_re, s0_im, tables, nb):
    m, bp, cp, lam_q, dcol = tables
    np_, _, qs, _, nbc = ut.shape
    nc = nbc // nb
    r = qs * S5_GROUP_CH
    pw = 2 * S5_STATE
    pack = lambda s: jnp.transpose(s.reshape(nb, np_, pw), (1, 0, 2))
    s0 = jnp.stack([pack(s0_re), pack(s0_im)], axis=1)
    kern = functools.partial(_s5_kernel, nb=nb, nc=nc)
    blk = lambda *shape: pl.BlockSpec((1,) + shape, lambda i: (i,) + (0,) * len(shape))
    yt, sout = pl.pallas_call(
        kern,
        grid=(np_,),
        in_specs=[blk(2, qs, S5_GROUP_CH, nbc), blk(2, r, r), blk(2 * r, 2 * pw), blk(2 * r, 2 * pw),
                  blk(2, pw), blk(2 * r, 1), blk(2, nb, pw)],
        out_specs=[blk(2, qs, S5_GROUP_CH, nbc), blk(2, nb, pw)],
        out_shape=[jax.ShapeDtypeStruct(ut.shape, BF16),
                   jax.ShapeDtypeStruct((np_, 2, nb, pw), F32)],
        scratch_shapes=[pltpu.VMEM((nbc, pw), F32)] * 4,
        compiler_params=_cparams("parallel"),
    )(ut, m, bp, cp, lam_q, dcol, s0)
    unpack = lambda s: jnp.transpose(s, (1, 0, 2)).reshape(nb, S5_GROUPS, S5_STATE)
    return yt, unpack(sout[:, 0]), unpack(sout[:, 1])


def _mix_kernel(ym_ref, yt_ref, x_ref, wglu_ref, bglu_ref, wmo_ref, g_ref, b_ref, o_ref):
    tn, slots, _ = x_ref.shape
    qb = yt_ref.shape[2]
    tr = lambda t: yt_ref[:, :, min(t, qb - 1)].reshape(D_S5, tn).astype(F32).T
    yg = jnp.concatenate([tr(t) for t in range(slots)], axis=0)
    ym = jnp.concatenate([ym_ref[:, t, :] for t in range(slots)], axis=0).astype(BF16)
    x = jnp.concatenate([x_ref[:, t, :] for t in range(slots)], axis=0)
    gate = _sigmoid(_dot(yg.astype(BF16), wglu_ref[...]) + bglu_ref[...])
    ys = (yg * gate).astype(BF16)
    mix = _dot(ym, wmo_ref[0:D_SSM, :]) + _dot(ys, wmo_ref[D_SSM:, :])
    out = _layer_norm(ALPHA * x + mix, g_ref[...], b_ref[...])
    for t in range(slots):
        o_ref[:, t, :] = out[t * tn:(t + 1) * tn]


def _mix(ym3, yt, x3, wglu, bglu, wmo, g, b, tn):
    nbc, q, _ = x3.shape
    np_, _, qs, _, _ = yt.shape
    qb = min(qs, SLOT_BLK)
    full = lambda t, j: (0, 0)
    tok = lambda t, j: (j, t, 0)
    return pl.pallas_call(
        _mix_kernel,
        grid=(q // SLOT_BLK, nbc // tn),
        in_specs=[pl.BlockSpec((tn, SLOT_BLK, D_SSM), tok),
                  pl.BlockSpec((np_, 2, qb, S5_GROUP_CH, tn), lambda t, j: (0, 0, jnp.minimum(t, qs // qb - 1), 0, j)),
                  pl.BlockSpec((tn, SLOT_BLK, D_MODEL), tok),
                  pl.BlockSpec(wglu.shape, full), pl.BlockSpec(bglu.shape, full),
                  pl.BlockSpec(wmo.shape, full), pl.BlockSpec(g.shape, full), pl.BlockSpec(b.shape, full)],
        out_specs=pl.BlockSpec((tn, SLOT_BLK, D_MODEL), tok),
        out_shape=jax.ShapeDtypeStruct((nbc, q, D_MODEL), F32),
        compiler_params=_cparams("parallel", "parallel"),
    )(ym3, yt, x3, wglu, bglu, wmo, g, b)


def _memkv_kernel(m_ref, wk_ref, wv_ref, k_ref, v_ref):
    m = m_ref[...].astype(BF16)
    k_ref[...] = _dot(m, wk_ref[...])
    v_ref[...] = _dot(m, wv_ref[...])


def _memkv(mem2d, wk, wv, tm):
    t = mem2d.shape[0]
    full = lambda i: (0, 0)
    row = lambda i: (i, 0)
    return pl.pallas_call(
        _memkv_kernel,
        grid=(t // tm,),
        in_specs=[pl.BlockSpec((tm, D_MODEL), row), pl.BlockSpec(wk.shape, full), pl.BlockSpec(wv.shape, full)],
        out_specs=[pl.BlockSpec((tm, D_MODEL), row), pl.BlockSpec((tm, D_MODEL), row)],
        out_shape=[jax.ShapeDtypeStruct((t, D_MODEL), F32), jax.ShapeDtypeStruct((t, D_MODEL), F32)],
        compiler_params=_cparams("parallel"),
    )(mem2d, wk, wv)


def _attn_kernel(x_ref, k_ref, v_ref, wq_ref, wo_ref, g_ref, b_ref, o_ref):
    x = x_ref[0]
    q = _dot(x.astype(BF16), wq_ref[...]).astype(BF16)
    k = k_ref[0].astype(BF16)
    v = v_ref[0].astype(BF16)
    scale = MEM_HEAD_DIM ** -0.5
    outs = []
    for h in range(MEM_HEADS):
        sl = slice(h * MEM_HEAD_DIM, (h + 1) * MEM_HEAD_DIM)
        s = _dot_nt(q[:, sl], k[:, sl]) * scale
        s = s - jnp.max(s, axis=-1, keepdims=True)
        p = jnp.exp(s)
        p = p / jnp.sum(p, axis=-1, keepdims=True)
        outs.append(_dot(p.astype(BF16), v[:, sl]))
    o = jnp.concatenate(outs, axis=1).astype(BF16)
    att = _dot(o, wo_ref[...])
    o_ref[0] = _layer_norm(ALPHA * x + att, g_ref[...], b_ref[...])


def _attn(x3d, k, v, wq, wo, g, b, tq):
    bsz, l, _ = x3d.shape
    m = k.shape[1]
    full = lambda i, j: (0, 0)
    seq = lambda i, j: (i, j, 0)
    per_b = lambda i, j: (i, 0, 0)
    return pl.pallas_call(
        _attn_kernel,
        grid=(bsz, l // tq),
        in_specs=[pl.BlockSpec((1, tq, D_MODEL), seq),
                  pl.BlockSpec((1, m, D_MODEL), per_b), pl.BlockSpec((1, m, D_MODEL), per_b),
                  pl.BlockSpec(wq.shape, full), pl.BlockSpec(wo.shape, full),
                  pl.BlockSpec(g.shape, full), pl.BlockSpec(b.shape, full)],
        out_specs=pl.BlockSpec((1, tq, D_MODEL), seq),
        out_shape=jax.ShapeDtypeStruct((bsz, l, D_MODEL), F32),
        compiler_params=_cparams("parallel", "parallel"),
    )(x3d, k, v, wq, wo, g, b)


def _softmax_rows(s):
    s = s - jnp.max(s, axis=-1, keepdims=True)
    p = jnp.exp(s)
    return p / jnp.sum(p, axis=-1, keepdims=True)


def _attn_cache_kernel(x_ref, k_ref, v_ref, wq_ref, wo_ref, g_ref, b_ref, o_ref):
    nb, l, _ = x_ref.shape
    x = x_ref[...].reshape(nb * l, D_MODEL)
    q = _dot(x.astype(BF16), wq_ref[...]).astype(BF16)
    scale = MEM_HEAD_DIM ** -0.5
    outs = []
    for i in range(nb):
        heads = []
        for h in range(MEM_HEADS):
            k = k_ref[0, i, :, h, :].astype(BF16)
            v = v_ref[0, i, :, h, :].astype(BF16)
            qh = q[i * l:(i + 1) * l, h * MEM_HEAD_DIM:(h + 1) * MEM_HEAD_DIM]
            p = _softmax_rows(_dot_nt(qh, k) * scale)
            heads.append(_dot(p.astype(BF16), v))
        outs.append(jnp.concatenate(heads, axis=1))
    o = jnp.concatenate(outs, axis=0).astype(BF16)
    att = _dot(o, wo_ref[...])
    o_ref[...] = _layer_norm(ALPHA * x + att, g_ref[...], b_ref[...]).reshape(nb, l, D_MODEL)


def _attn_cache(x3d, k5, v5, wq, wo, g, b, nb):
    bsz, l, _ = x3d.shape
    full = lambda i: (0, 0)
    kv_blk = pl.BlockSpec((1, nb) + k5.shape[2:], lambda i: (0, i, 0, 0, 0))
    return pl.pallas_call(
        _attn_cache_kernel,
        grid=(bsz // nb,),
        in_specs=[pl.BlockSpec((nb, l, D_MODEL), lambda i: (i, 0, 0)), kv_blk, kv_blk,
                  pl.BlockSpec(wq.shape, full), pl.BlockSpec(wo.shape, full),
                  pl.BlockSpec(g.shape, full), pl.BlockSpec(b.shape, full)],
        out_specs=pl.BlockSpec((nb, l, D_MODEL), lambda i: (i, 0, 0)),
        out_shape=jax.ShapeDtypeStruct((bsz, l, D_MODEL), F32),
        compiler_params=_cparams("parallel"),
    )(x3d, k5, v5, wq, wo, g, b)


def _router_kernel(x_ref, wh_ref, wl_ref, br_ref, comb_ref, pos_ref, cnt_ref):
    x = x_ref[...]
    tm = x.shape[0]
    xh, xl = _split(x)
    wh = wh_ref[...]
    logits = _dot_nt(wh, xh) + _dot_nt(wh, xl) + _dot_nt(wl_ref[...], xh)
    s = _sigmoid(logits)
    sb = s + br_ref[...]
    per = N_EXPERTS // N_EXPERT_GROUPS
    neg = -jnp.inf

    sb3 = sb.reshape(N_EXPERT_GROUPS, per, tm)
    mem = lax.broadcasted_iota(jnp.int32, sb3.shape, 1)
    m1 = jnp.max(sb3, axis=1, keepdims=True)
    first = jnp.min(jnp.where(sb3 == m1, mem, per), axis=1, keepdims=True)
    m2 = jnp.max(jnp.where(mem == first, neg, sb3), axis=1, keepdims=True)
    gs = (m1 + m2).reshape(N_EXPERT_GROUPS, tm)

    def rank_rows(v):
        n = v.shape[0]
        idx = lax.broadcasted_iota(jnp.int32, v.shape, 0)
        rank = jnp.zeros(v.shape, jnp.int32)
        for j in range(n):
            rj = v[j:j + 1, :]
            beats = (rj > v) | ((rj == v) & (idx > j))
            rank = rank + beats.astype(jnp.int32)
        return rank

    gsel = rank_rows(gs) < TOPK_GROUPS
    emask = jnp.broadcast_to(gsel.reshape(N_EXPERT_GROUPS, 1, tm), sb3.shape).reshape(N_EXPERTS, tm)
    sm = jnp.where(emask, sb, neg)
    sel = (rank_rows(sm) < TOP_K) & emask
    gates = jnp.where(sel, s, 0.0)
    denom = jnp.sum(gates, axis=0, keepdims=True)
    comb_ref[...] = gates / denom * ROUTE_SCALE

    self = jnp.where(sel, 1.0, 0.0)
    before = (lax.broadcasted_iota(jnp.int32, (tm, tm), 0)
              < lax.broadcasted_iota(jnp.int32, (tm, tm), 1)).astype(BF16)
    rank_tok = _dot(self.astype(BF16), before)
    pos_ref[...] = jnp.where(sel, rank_tok.astype(jnp.int32), -1)
    cnt = jnp.sum(self, axis=1, keepdims=True).astype(jnp.int32)
    cnt_ref[0] = jnp.broadcast_to(cnt, cnt_ref.shape[1:])


def _router(x2d, wr_hi_t, wr_lo_t, br_col, tm):
    t = x2d.shape[0]
    nt = t // tm
    full = lambda i: (0, 0)
    return pl.pallas_call(
        _router_kernel,
        grid=(nt,),
        in_specs=[pl.BlockSpec((tm, D_MODEL), lambda i: (i, 0)), pl.BlockSpec(wr_hi_t.shape, full),
                  pl.BlockSpec(wr_lo_t.shape, full), pl.BlockSpec(br_col.shape, full)],
        out_specs=[pl.BlockSpec((N_EXPERTS, tm), lambda i: (0, i)),
                   pl.BlockSpec((N_EXPERTS, tm), lambda i: (0, i)),
                   pl.BlockSpec((1, N_EXPERTS, 128), lambda i: (i, 0, 0))],
        out_shape=[jax.ShapeDtypeStruct((N_EXPERTS, t), F32),
                   jax.ShapeDtypeStruct((N_EXPERTS, t), jnp.int32),
                   jax.ShapeDtypeStruct((nt, N_EXPERTS, 128), jnp.int32)],
        compiler_params=_cparams("parallel"),
    )(x2d, wr_hi_t, wr_lo_t, br_col)


def _moe_rows(t, nt):
    worst = TOP_K * t + nt * N_EXPERTS * (MOE_SEG - 1) + N_EXPERTS * (MOE_TR - 1)
    return -(-worst // MOE_TR) * MOE_TR


def _moe_local_rows(tm):
    worst = TOP_K * tm + N_EXPERTS * (MOE_SEG - 1)
    return -(-worst // 512) * 512


def _moe_plan(cnt, tm):
    nt = cnt.shape[0]
    npad = (cnt + (MOE_SEG - 1)) // MOE_SEG * MOE_SEG
    loc_off = jnp.cumsum(npad, axis=1) - npad
    tot = jnp.sum(npad, axis=0)
    totp = (tot + (MOE_TR - 1)) // MOE_TR * MOE_TR
    gend = jnp.cumsum(totp)
    goff = gend - totp
    seg_off = goff[None, :] + jnp.cumsum(npad, axis=0) - npad
    n_row_tiles = _moe_rows(nt * tm, nt) // MOE_TR
    tile_start = jnp.arange(n_row_tiles, dtype=jnp.int32) * MOE_TR
    ended = (gend[None, :] <= tile_start[:, None]).astype(jnp.int32)
    tile_expert = jnp.minimum(jnp.sum(ended, axis=1), N_EXPERTS - 1)
    n_valid = (gend[-1] // MOE_TR).reshape(1)
    i32 = lambda a: a.astype(jnp.int32)
    return dict(npad=i32(npad.reshape(-1)), loc_off=i32(loc_off.reshape(-1)), seg_off=i32(seg_off.reshape(-1)),
                fill_off=i32(goff + tot), fill_n=i32((totp - tot) // MOE_SEG),
                tile_expert=i32(tile_expert), n_valid=i32(n_valid))


def _for_each_segment(i, npad_ref, loc_ref, seg_ref, body):
    def per_expert(e, n):
        k = i * N_EXPERTS + e
        nch = npad_ref[k] // MOE_SEG
        base = loc_ref[k]
        dst = seg_ref[k]

        def chunk(j, c):
            body(e, j, pl.multiple_of(base + j * MOE_SEG, MOE_SEG), pl.multiple_of(dst + j * MOE_SEG, MOE_SEG))
            return c

        lax.fori_loop(0, nch, chunk, 0)
        return n + nch

    return lax.fori_loop(0, N_EXPERTS, per_expert, 0)


def _one_hot_rows(pos_ref, e, j, tm, value):
    row = pos_ref[pl.ds(e, 1), :]
    want = lax.broadcasted_iota(jnp.int32, (MOE_SEG, tm), 0) + j * MOE_SEG
    return jnp.where(row == want, value, 0.0).astype(BF16)


def _moe_gather_kernel(npad_ref, loc_ref, seg_ref, foff_ref, fn_ref, x_ref, pos_ref, xs_hbm,
                       p_scr, xs_scr, zero_scr, sem):
    i = pl.program_id(0)
    tm = x_ref.shape[0]
    p_scr[...] = jnp.zeros_like(p_scr)

    def put(e, j, lrow, grow):
        p_scr[pl.ds(lrow, MOE_SEG), :] = _one_hot_rows(pos_ref, e, j, tm, 1.0)

    _for_each_segment(i, npad_ref, loc_ref, seg_ref, put)
    xs_scr[...] = _dot(p_scr[...], x_ref[...].astype(BF16)).astype(BF16)

    def send(e, j, lrow, grow):
        pltpu.make_async_copy(xs_scr.at[pl.ds(lrow, MOE_SEG)], xs_hbm.at[pl.ds(grow, MOE_SEG)], sem).start()

    n_sent = _for_each_segment(i, npad_ref, loc_ref, seg_ref, send)

    def wait_one(k, c):
        pltpu.make_async_copy(xs_scr.at[pl.ds(0, MOE_SEG)], xs_hbm.at[pl.ds(0, MOE_SEG)], sem).wait()
        return c

    lax.fori_loop(0, n_sent, wait_one, 0)

    @pl.when(i == pl.num_programs(0) - 1)
    def _():
        zero_scr[...] = jnp.zeros_like(zero_scr)

        def per_expert(e, n):
            def chunk(j, c):
                grow = pl.multiple_of(foff_ref[e] + j * MOE_SEG, MOE_SEG)
                pltpu.make_async_copy(zero_scr, xs_hbm.at[pl.ds(grow, MOE_SEG)], sem).start()
                return c
            lax.fori_loop(0, fn_ref[e], chunk, 0)
            return n + fn_ref[e]

        n_fill = lax.fori_loop(0, N_EXPERTS, per_expert, 0)
        lax.fori_loop(0, n_fill, wait_one, 0)


def _moe_gather(plan, x2d, pos, tm):
    t = x2d.shape[0]
    nt = t // tm
    rows = _moe_rows(t, nt)
    rl = _moe_local_rows(tm)
    return pl.pallas_call(
        _moe_gather_kernel,
        grid_spec=pltpu.PrefetchScalarGridSpec(
            num_scalar_prefetch=5,
            grid=(nt,),
            in_specs=[pl.BlockSpec((tm, D_MODEL), lambda i, *_: (i, 0)),
                      pl.BlockSpec((N_EXPERTS, tm), lambda i, *_: (0, i))],
            out_specs=pl.BlockSpec(memory_space=pl.ANY),
            scratch_shapes=[pltpu.VMEM((rl, tm), BF16), pltpu.VMEM((rl, D_MODEL), BF16),
                            pltpu.VMEM((MOE_SEG, D_MODEL), BF16), pltpu.SemaphoreType.DMA]),
        out_shape=jax.ShapeDtypeStruct((rows, D_MODEL), BF16),
        compiler_params=_cparams("arbitrary"),
    )(plan['npad'], plan['loc_off'], plan['seg_off'], plan['fill_off'], plan['fill_n'], x2d, pos)


def _moe_expert_kernel(te_ref, nv_ref, xs_ref, wg_ref, wu_ref, wd_ref, ys_ref, wg_s, wu_s, wd_s):
    j = pl.program_id(0)
    changed = te_ref[j] != te_ref[jnp.maximum(j - 1, 0)]

    @pl.when((j == 0) | changed)
    def _():
        wg_s[...] = wg_ref[0].astype(BF16)
        wu_s[...] = wu_ref[0].astype(BF16)
        wd_s[...] = wd_ref[0].astype(BF16)

    @pl.when(j < nv_ref[0])
    def _():
        x = xs_ref[...]
        hid = _silu(_dot(x, wg_s[...])) * _dot(x, wu_s[...])
        ys_ref[...] = _dot(hid.astype(BF16), wd_s[...]).astype(ys_ref.dtype)


def _moe_experts(plan, xs, wg, wu, wd):
    rows = xs.shape[0]
    de = wg.shape[2]
    row_tile = lambda j, te, nv: (jnp.minimum(j, nv[0] - 1), 0)
    expert = lambda j, te, nv: (te[jnp.minimum(j, nv[0] - 1)], 0, 0)
    return pl.pallas_call(
        _moe_expert_kernel,
        grid_spec=pltpu.PrefetchScalarGridSpec(
            num_scalar_prefetch=2,
            grid=(rows // MOE_TR,),
            in_specs=[pl.BlockSpec((MOE_TR, D_MODEL), row_tile),
                      pl.BlockSpec((1, D_MODEL, de), expert), pl.BlockSpec((1, D_MODEL, de), expert),
                      pl.BlockSpec((1, de, D_MODEL), expert)],
            out_specs=pl.BlockSpec((MOE_TR, D_MODEL), row_tile),
            scratch_shapes=[pltpu.VMEM((D_MODEL, de), BF16), pltpu.VMEM((D_MODEL, de), BF16),
                            pltpu.VMEM((de, D_MODEL), BF16)]),
        out_shape=jax.ShapeDtypeStruct((rows, D_MODEL), BF16),
        compiler_params=_cparams("arbitrary"),
    )(plan['tile_expert'], plan['n_valid'], xs, wg, wu, wd)


def _moe_combine_kernel(npad_ref, loc_ref, seg_ref, x_ref, pos_ref, comb_ref, ys_hbm, sg_ref, su_ref,
                        sd_ref, g_ref, b_ref, o_ref, pg_scr, ys_scr, sem):
    i = pl.program_id(0)
    tm = x_ref.shape[0]

    @pl.when(i == 0)
    def _():
        ys_scr[...] = jnp.zeros_like(ys_scr)

    def fetch(e, j, lrow, grow):
        pltpu.make_async_copy(ys_hbm.at[pl.ds(grow, MOE_SEG)], ys_scr.at[pl.ds(lrow, MOE_SEG)], sem).start()

    n_fetch = _for_each_segment(i, npad_ref, loc_ref, seg_ref, fetch)

    pg_scr[...] = jnp.zeros_like(pg_scr)

    def put(e, j, lrow, grow):
        pg_scr[pl.ds(lrow, MOE_SEG), :] = _one_hot_rows(pos_ref, e, j, tm, comb_ref[pl.ds(e, 1), :])

    _for_each_segment(i, npad_ref, loc_ref, seg_ref, put)

    x = x_ref[...]
    xb = x.astype(BF16)
    hs = _silu(_dot(xb, sg_ref[...])) * _dot(xb, su_ref[...])
    shared = _dot(hs.astype(BF16), sd_ref[...])

    def wait_one(k, c):
        pltpu.make_async_copy(ys_hbm.at[pl.ds(0, MOE_SEG)], ys_scr.at[pl.ds(0, MOE_SEG)], sem).wait()
        return c

    lax.fori_loop(0, n_fetch, wait_one, 0)
    routed = _dot_tn(pg_scr[...], ys_scr[...])
    o_ref[...] = _layer_norm(ALPHA * x + (routed + shared), g_ref[...], b_ref[...])


def _moe_combine(plan, x2d, pos, comb, ys, sg, su, sd, g, b, tm):
    t = x2d.shape[0]
    nt = t // tm
    rl = _moe_local_rows(tm)
    full = lambda i, *_: (0, 0)
    return pl.pallas_call(
        _moe_combine_kernel,
        grid_spec=pltpu.PrefetchScalarGridSpec(
            num_scalar_prefetch=3,
            grid=(nt,),
            in_specs=[pl.BlockSpec((tm, D_MODEL), lambda i, *_: (i, 0)),
                      pl.BlockSpec((N_EXPERTS, tm), lambda i, *_: (0, i)),
                      pl.BlockSpec((N_EXPERTS, tm), lambda i, *_: (0, i)),
                      pl.BlockSpec(memory_space=pl.ANY),
                      pl.BlockSpec(sg.shape, full), pl.BlockSpec(su.shape, full), pl.BlockSpec(sd.shape, full),
                      pl.BlockSpec(g.shape, full), pl.BlockSpec(b.shape, full)],
            out_specs=pl.BlockSpec((tm, D_MODEL), lambda i, *_: (i, 0)),
            scratch_shapes=[pltpu.VMEM((rl, tm), BF16), pltpu.VMEM((rl, D_MODEL), BF16),
                            pltpu.SemaphoreType.DMA]),
        out_shape=jax.ShapeDtypeStruct((t, D_MODEL), F32),
        compiler_params=_cparams("arbitrary"),
    )(plan['npad'], plan['loc_off'], plan['seg_off'], x2d, pos, comb, ys, sg, su, sd, g, b)


def _moe(x2d, w, tm):
    comb, pos, cnt = _router(x2d, w['wr_hi_t'], w['wr_lo_t'], w['br_col'], tm)
    plan = _moe_plan(cnt[:, :, 0], tm)
    xs = _moe_gather(plan, x2d, pos, tm)
    ys = _moe_experts(plan, xs, w['w_e_gate'], w['w_e_up'], w['w_e_down'])
    return _moe_combine(plan, x2d, pos, comb, ys, w['w_s_gate'], w['w_s_up'], w['w_s_down'],
                        w['ln3_g'], w['ln3_b'], tm)


def _layer(x3d, mem_k, mem_v, conv_st, ssm_st, s5_re, s5_im, w, *, l_valid, s5_chunk, tm, tq):
    bsz, lp, _ = x3d.shape
    t = bsz * lp
    x2d = x3d.reshape(t, D_MODEL)
    tm = min(tm, t)
    tq = min(tq, lp)
    z, xbc, dt = _in_proj(x2d, w['wz'], w['wx'], w['wd'], tm)
    q_in = min(lp, SSD_CHUNK)
    h0_t = jnp.transpose(ssm_st.reshape(bsz, D_SSM, SSM_STATE), (0, 2, 1))
    ym, conv_new, h_t = _ssd(xbc.reshape(bsz, lp, D_CONV), z.reshape(bsz, lp, D_SSM),
                             dt.reshape(bsz, lp, SSM_HEADS), conv_st, h0_t,
                             w['conv_w'], w['conv_b'], w['dt_bias'], w['a_log'], w['d_skip_x'],
                             w['norm_ssm'], q_in, min(l_valid, q_in))
    ssm_new = jnp.transpose(h_t, (0, 2, 1)).reshape(bsz, SSM_HEADS, SSM_HEAD_DIM, SSM_STATE)
    n_chunks = l_valid // s5_chunk
    nbc = bsz * n_chunks
    slots = lp // n_chunks
    tn = min(SLOT_TN, nbc)
    x3 = x3d.reshape(nbc, slots, D_MODEL)
    ut = _in_proj_u(x3, w['wu_t'], s5_chunk, tn)
    yt, s5_re_new, s5_im_new = _s5(ut, s5_re, s5_im, w['s5_tables'][s5_chunk], bsz)
    x1 = _mix(ym.reshape(nbc, slots, D_SSM), yt, x3, w['w_glu'], w['b_glu'], w['w_mix_out'],
              w['ln1_g'], w['ln1_b'], tn)
    attn_args = (x1.reshape(bsz, lp, D_MODEL), mem_k, mem_v, w['w_q'], w['w_o'], w['ln2_g'], w['ln2_b'])
    if mem_k.ndim == 5:
        x2 = _attn_cache(*attn_args, min(ATTN_CACHE_NB, bsz))
    else:
        x2 = _attn(*attn_args, tq)
    return x2.reshape(t, D_MODEL), conv_new, ssm_new, s5_re_new, s5_im_new


def kernel(x_prompt, x_sample, mem_prompt, state_conv, state_ssm, state_s5_re, state_s5_im, cache_mem_k, cache_mem_v, w_in, conv_w, conv_b, dt_bias, a_log, d_ssm, norm_ssm, s5_a_re, s5_a_im, s5_log_step, s5_b_re, s5_b_im, s5_c_re, s5_c_im, s5_d, w_glu, b_glu, w_mix_out, ln1_g, ln1_b, w_q, w_k, w_v, w_o, ln2_g, ln2_b, w_router, b_router, w_e_gate, w_e_up, w_e_down, w_s_gate, w_s_up, w_s_down, ln3_g, ln3_b):
    assert w_in.shape[0] == DEPTH
    bp, lp, _ = x_prompt.shape
    bs, ls, _ = x_sample.shape
    row = lambda a: a.reshape(1, -1)
    bf = lambda a: a.astype(BF16)
    w_in0 = w_in[0]
    o1 = D_SSM
    o2 = o1 + D_CONV
    o3 = o2 + SSM_HEADS
    wr_hi, wr_lo = _split(w_router[0].T)
    s5_args = (s5_a_re[0], s5_a_im[0], s5_log_step[0], s5_b_re[0], s5_b_im[0], s5_c_re[0], s5_c_im[0], s5_d[0])
    s5_chunk_p = math.gcd(lp, 16)
    w = {
        'wz': bf(w_in0[:, :o1]), 'wx': bf(w_in0[:, o1:o2]), 'wd': bf(w_in0[:, o2:o3]), 'wu_t': bf(w_in0[:, o3:].T),
        'conv_w': conv_w[0], 'conv_b': row(conv_b[0]), 'dt_bias': row(dt_bias[0]), 'a_log': row(a_log[0]),
        'd_skip_x': row(jnp.repeat(d_ssm[0], SSM_HEAD_DIM)), 'norm_ssm': row(norm_ssm[0]),
        's5_tables': {q: _s5_tables(*s5_args, q) for q in sorted({s5_chunk_p, ls})},
        'w_glu': bf(w_glu[0]), 'b_glu': row(b_glu[0]), 'w_mix_out': bf(w_mix_out[0]),
        'ln1_g': row(ln1_g[0]), 'ln1_b': row(ln1_b[0]),
        'w_q': bf(w_q[0]), 'w_o': bf(w_o[0]), 'ln2_g': row(ln2_g[0]), 'ln2_b': row(ln2_b[0]),
        'wr_hi_t': wr_hi, 'wr_lo_t': wr_lo, 'br_col': b_router[0].reshape(-1, 1),
        'w_e_gate': w_e_gate[0], 'w_e_up': w_e_up[0], 'w_e_down': w_e_down[0],
        'w_s_gate': bf(w_s_gate[0]), 'w_s_up': bf(w_s_up[0]), 'w_s_down': bf(w_s_down[0]),
        'ln3_g': row(ln3_g[0]), 'ln3_b': row(ln3_b[0]),
    }

    m_tok = mem_prompt.shape[1]
    mk2d, mv2d = _memkv(mem_prompt.reshape(bp * m_tok, D_MODEL), bf(w_k[0]), bf(w_v[0]), 512)
    mk = mk2d.reshape(bp, m_tok, D_MODEL)
    mv = mv2d.reshape(bp, m_tok, D_MODEL)
    zeros = lambda *s: jnp.zeros(s, F32)
    x2p, conv_p, ssm_p, s5r_p, s5i_p = _layer(
        x_prompt, mk, mv, zeros(bp, CONV_W - 1, D_CONV), zeros(bp, SSM_HEADS, SSM_HEAD_DIM, SSM_STATE),
        zeros(bp, S5_GROUPS, S5_STATE), zeros(bp, S5_GROUPS, S5_STATE), w,
        l_valid=lp, s5_chunk=s5_chunk_p, tm=512, tq=512)

    xs_pad = jnp.pad(x_sample, ((0, 0), (0, SAMPLE_PAD - ls), (0, 0)))
    x2s, conv_s, ssm_s, s5r_s, s5i_s = _layer(
        xs_pad, cache_mem_k, cache_mem_v, state_conv[0], state_ssm[0], state_s5_re[0], state_s5_im[0], w,
        l_valid=ls, s5_chunk=ls, tm=512, tq=SAMPLE_PAD)

    tp = bp * lp
    y_all = _moe(jnp.concatenate([x2p, x2s], axis=0), w, MOE_TM)
    yp = y_all[:tp].reshape(bp, lp, D_MODEL)
    ys = y_all[tp:].reshape(bs, SAMPLE_PAD, D_MODEL)[:, :ls]

    kv_shape = (1, bp, m_tok, MEM_HEADS, MEM_HEAD_DIM)
    return (yp, ys, conv_p[None], ssm_p[None], s5r_p[None], s5i_p[None],
            mk.reshape(kv_shape), mv.reshape(kv_shape),
            conv_s[None], ssm_s[None], s5r_s[None], s5i_s[None])
```
